```python
import math
import jax, jax.numpy as jnp
from jax import lax
import numpy as np

D_MODEL = 1024
BATCH = 32
SEQ = 2048
DEPTH = 1

D_MIX = D_MODEL
NSA_HEADS = 8
NSA_KV_GROUPS = 2
NSA_HEAD_DIM = 64
NSA_HPG = NSA_HEADS // NSA_KV_GROUPS
CMP_BLOCK = 32
CMP_STRIDE = 16
CMP_HIDDEN = 256
SLC_BLOCK = 64
SLC_TOPK = 8
WINDOW = 512
FORCE_SCORE = 1e4
GLA_HEADS = 4
GLA_DK = 64
GLA_DV = 128
GLA_GATE_RANK = 16
GLA_TAU = 16.0
GLA_CHUNK = 64
ROPE_THETA = 500000.0
ROT_DIM = NSA_HEAD_DIM // 4
D_FF = 2816
CONV_W = 3
EPS = 1e-6
Q_BLOCK = 64
NEG = -1e30

NSA_Q_W = NSA_HEADS * NSA_HEAD_DIM
NSA_KV_W = NSA_KV_GROUPS * NSA_HEAD_DIM
GLA_QK_W = GLA_HEADS * GLA_DK
GLA_V_W = GLA_HEADS * GLA_DV
IN_SPLITS = [NSA_Q_W, 6 * NSA_KV_W, 3 * NSA_HEADS, GLA_QK_W, GLA_QK_W, GLA_V_W, GLA_GATE_RANK, GLA_V_W]
D_IN = sum(IN_SPLITS)

kernel_name = 'hymba_nsa_gla_convffn_block'


def rms_norm(x, g):
    xf = x.astype(jnp.float32)
    y = xf * lax.rsqrt(jnp.mean(xf * xf, axis=-1, keepdims=True) + EPS)
    return (y * g.astype(jnp.float32)).astype(x.dtype)


def partial_rope(x, pos):
    half = ROT_DIM // 2
    inv = ROPE_THETA ** (-jnp.arange(half, dtype=jnp.float32) / half)
    ang = pos.astype(jnp.float32)[:, None] * inv[None, :]
    cos = jnp.cos(ang)[:, None, :]
    sin = jnp.sin(ang)[:, None, :]
    xr = x[..., :ROT_DIM].astype(jnp.float32)
    x1, x2 = xr[..., :half], xr[..., half:]
    rot = jnp.concatenate([x1 * cos - x2 * sin, x2 * cos + x1 * sin], axis=-1).astype(x.dtype)
    return jnp.concatenate([rot, x[..., ROT_DIM:]], axis=-1)


def masked_softmax(s, mask):
    s = jnp.where(mask, s.astype(jnp.float32), NEG)
    return jax.nn.softmax(s, axis=-1) * mask


def nsa_mixer(q, kv, gate_logits, q_gain, k_gains, pe_k, pe_v, w1_k, w2_k, w1_v, w2_v):
    B, S = q.shape[0], q.shape[1]
    dt = q.dtype
    G, HPG, D = NSA_KV_GROUPS, NSA_HPG, NSA_HEAD_DIM
    scale = D ** -0.5
    pos = jnp.arange(S)
    kc, vc, ks, vs, kw, vw = [c.reshape(B, S, G, D) for c in jnp.split(kv, 6, axis=-1)]
    q = partial_rope(rms_norm(q.reshape(B, S, NSA_HEADS, D), q_gain), pos)
    q = q.reshape(B, S, G, HPG, D) * scale

    n_cmp = (S - CMP_BLOCK) // CMP_STRIDE + 1
    blk = np.arange(n_cmp)[:, None] * CMP_STRIDE + np.arange(CMP_BLOCK)[None, :]

    def compress(t, pe, w1, w2):
        tb = t[:, blk] + pe[:, None, :]
        tb = jnp.moveaxis(tb, 3, 2).reshape(B, n_cmp, G, CMP_BLOCK * D)
        return jax.nn.gelu(tb @ w1) @ w2

    cmp_end = jnp.arange(n_cmp) * CMP_STRIDE + CMP_BLOCK - 1
    k_cmp = partial_rope(rms_norm(compress(kc, pe_k, w1_k, w2_k), k_gains[0]), cmp_end)
    v_cmp = compress(vc, pe_v, w1_v, w2_v)
    s_cmp = jnp.einsum('bsghd,bngd->bghsn', q, k_cmp)
    p_cmp = masked_softmax(s_cmp, cmp_end[None, :] <= pos[:, None])
    o_cmp = jnp.einsum('bghsn,bngd->bsghd', p_cmp.astype(dt), v_cmp).reshape(B, S, NSA_HEADS, D)

    n_slc = S // SLC_BLOCK
    top_k = min(SLC_TOPK, n_slc)
    ci = np.arange(n_cmp)[:, None]
    sj = np.arange(n_slc)[None, :]
    overlap = ((ci * CMP_STRIDE < (sj + 1) * SLC_BLOCK) &
               (ci * CMP_STRIDE + CMP_BLOCK > sj * SLC_BLOCK)).astype(np.float32)
    p_slc = jnp.einsum('bghsn,nj->bgsj', p_cmp, jnp.asarray(overlap))
    j = jnp.arange(n_slc)[None, :]
    cur = (pos // SLC_BLOCK)[:, None]
    forced = (j == 0) | (j == cur) | (j == cur - 1)
    causal_blk = j * SLC_BLOCK <= pos[:, None]
    score = jnp.where(causal_blk, jnp.where(forced, FORCE_SCORE, p_slc), -jnp.inf)
    _, sel_idx = lax.top_k(score, top_k)

    ks = partial_rope(rms_norm(ks, k_gains[1]), pos)
    kw = partial_rope(rms_norm(kw, k_gains[2]), pos)
    ks_blk = jnp.moveaxis(ks.reshape(B, n_slc, SLC_BLOCK, G, D), 3, 1)
    vs_blk = jnp.moveaxis(vs.reshape(B, n_slc, SLC_BLOCK, G, D), 3, 1)
    pad = ((0, 0), (WINDOW, 0), (0, 0), (0, 0))
    kw_pad = jnp.pad(kw, pad)
    vw_pad = jnp.pad(vw, pad)
    n_qb = S // Q_BLOCK
    q_blocks = jnp.moveaxis(q.reshape(B, n_qb, Q_BLOCK, G, HPG, D), 1, 0)
    idx_blocks = jnp.moveaxis(sel_idx.reshape(B, G, n_qb, Q_BLOCK, top_k), 2, 0)
    bi = jnp.arange(B)[:, None, None, None]
    gi = jnp.arange(G)[None, :, None, None]

    def block_fn(args):
        i, qb, ib = args
        start = i * Q_BLOCK
        tq = start + jnp.arange(Q_BLOCK)
        k_sel = ks_blk[bi, gi, ib].reshape(B, G, Q_BLOCK, top_k * SLC_BLOCK, D)
        v_sel = vs_blk[bi, gi, ib].reshape(B, G, Q_BLOCK, top_k * SLC_BLOCK, D)
        kpos = (ib[..., None] * SLC_BLOCK + jnp.arange(SLC_BLOCK)).reshape(B, G, Q_BLOCK, top_k * SLC_BLOCK)
        m_sel = kpos <= tq[None, None, :, None]
        s = jnp.einsum('bqghd,bgqkd->bghqk', qb, k_sel)
        p = masked_softmax(s, m_sel[:, :, None])
        o_sel = jnp.einsum('bghqk,bgqkd->bqghd', p.astype(dt), v_sel)
        k_win = lax.dynamic_slice_in_dim(kw_pad, start, WINDOW + Q_BLOCK, axis=1)
        v_win = lax.dynamic_slice_in_dim(vw_pad, start, WINDOW + Q_BLOCK, axis=1)
        wpos = start - WINDOW + jnp.arange(WINDOW + Q_BLOCK)
        m_win = ((wpos[None, :] <= tq[:, None]) & (wpos[None, :] > tq[:, None] - WINDOW)
                 & (wpos[None, :] >= 0))
        s = jnp.einsum('bqghd,bkgd->bghqk', qb, k_win)
        p = masked_softmax(s, m_win)
        o_win = jnp.einsum('bghqk,bkgd->bqghd', p.astype(dt), v_win)
        return o_sel, o_win

    o_sel, o_win = lax.map(block_fn, (jnp.arange(n_qb), q_blocks, idx_blocks))
    o_sel = jnp.moveaxis(o_sel, 0, 1).reshape(B, S, NSA_HEADS, D)
    o_win = jnp.moveaxis(o_win, 0, 1).reshape(B, S, NSA_HEADS, D)

    g = jax.nn.sigmoid(gate_logits.reshape(B, S, NSA_HEADS, 3).astype(jnp.float32)).astype(dt)
    o = g[..., 0:1] * o_cmp + g[..., 1:2] * o_sel + g[..., 2:3] * o_win
    return o.reshape(B, S, NSA_Q_W)


def gla_mixer(q, k, v, g_lr, g_out, w_gate2, b_gate, out_gain):
    B, S = q.shape[0], q.shape[1]
    dt = q.dtype
    H, L = GLA_HEADS, GLA_CHUNK
    C = S // L

    def heads(t, d):
        return t.astype(jnp.float32).reshape(B, C, L, H, d).transpose(0, 3, 1, 2, 4)

    log_a = jax.nn.log_sigmoid((g_lr @ w_gate2 + b_gate).astype(jnp.float32)) / GLA_TAU
    qh = heads(q, GLA_DK) * (GLA_DK ** -0.5)
    kh = heads(k, GLA_DK)
    vh = heads(v, GLA_DV)
    b = jnp.cumsum(heads(log_a, GLA_DK), axis=3)
    b_last = b[:, :, :, -1:, :]
    q_t = qh * jnp.exp(b)
    k_t = kh * jnp.exp(-b)
    tril = jnp.tril(jnp.ones((L, L), dtype=bool))
    A = jnp.where(tril, jnp.einsum('bhcid,bhcjd->bhcij', q_t, k_t), 0.0)
    o_intra = jnp.einsum('bhcij,bhcjv->bhciv', A, vh)
    dS = jnp.einsum('bhcjd,bhcjv->bhcdv', kh * jnp.exp(b_last - b), vh)
    decay = jnp.exp(b_last[:, :, :, 0, :])

    def step(state, xs):
        dec, ds = xs
        return dec[..., None] * state + ds, state

    init = jnp.zeros((B, H, GLA_DK, GLA_DV), jnp.float32)
    _, s_prev = lax.scan(step, init, (jnp.moveaxis(decay, 2, 0), jnp.moveaxis(dS, 2, 0)))
    s_prev = jnp.moveaxis(s_prev, 0, 2)
    o = o_intra + jnp.einsum('bhcid,bhcdv->bhciv', q_t, s_prev)
    o = o.transpose(0, 2, 3, 1, 4).reshape(B, S, H, GLA_DV)
    o = rms_norm(o, out_gain) * jax.nn.silu(g_out.astype(jnp.float32).reshape(B, S, H, GLA_DV))
    return o.reshape(B, S, GLA_V_W).astype(dt)


def causal_dwconv(u, w, b):
    C = u.shape[-1]
    y = lax.conv_general_dilated(u, w[:, None, :].astype(u.dtype), window_strides=(1,),
                                 padding=[(CONV_W - 1, 0)],
                                 dimension_numbers=('NWC', 'WIO', 'NWC'),
                                 feature_group_count=C)
    return y + b


def setup_inputs(seed: int = 0) -> dict:
    key = jax.random.key(seed)
    ks = jax.random.split(key, 24)
    L = DEPTH

    def nrm(k, shape, scale):
        return jax.random.normal(k, shape, jnp.float32) * scale

    def gain(k, shape):
        return 1.0 + 0.05 * jax.random.normal(k, shape, jnp.float32)

    cmp_in = CMP_BLOCK * NSA_HEAD_DIM
    return {
        'x': nrm(ks[0], (BATCH, SEQ, D_MODEL), 1.0),
        'attn_norm': gain(ks[1], (L, D_MODEL)),
        'w_in': nrm(ks[2], (L, D_MODEL, D_IN), D_MODEL ** -0.5),
        'nsa_q_norm': gain(ks[3], (L, NSA_HEAD_DIM)),
        'nsa_k_norm': gain(ks[4], (L, 3, NSA_HEAD_DIM)),
        'cmp_pos_k': nrm(ks[5], (L, CMP_BLOCK, NSA_HEAD_DIM), 0.1),
        'cmp_pos_v': nrm(ks[6], (L, CMP_BLOCK, NSA_HEAD_DIM), 0.1),
        'cmp_w1_k': nrm(ks[7], (L, cmp_in, CMP_HIDDEN), cmp_in ** -0.5),
        'cmp_w2_k': nrm(ks[8], (L, CMP_HIDDEN, NSA_HEAD_DIM), CMP_HIDDEN ** -0.5),
        'cmp_w1_v': nrm(ks[9], (L, cmp_in, CMP_HIDDEN), cmp_in ** -0.5),
        'cmp_w2_v': nrm(ks[10], (L, CMP_HIDDEN, NSA_HEAD_DIM), CMP_HIDDEN ** -0.5),
        'gla_w_gate2': nrm(ks[11], (L, GLA_GATE_RANK, GLA_QK_W), GLA_GATE_RANK ** -0.5),
        'gla_b_gate': nrm(ks[12], (L, GLA_QK_W), 0.1),
        'gla_out_norm': gain(ks[13], (L, GLA_DV)),
        'w_out': nrm(ks[14], (L, D_MIX, D_MODEL), D_MIX ** -0.5),
        'ffn_norm': gain(ks[15], (L, D_MODEL)),
        'w_up': nrm(ks[16], (L, D_MODEL, 2 * D_FF), D_MODEL ** -0.5),
        'conv_w': nrm(ks[17], (L, CONV_W, 2 * D_FF), CONV_W ** -0.5),
        'conv_b': nrm(ks[18], (L, 2 * D_FF), 0.02),
        'w_down': nrm(ks[19], (L, D_FF, D_MODEL), D_FF ** -0.5),
    }


def reference(x, attn_norm, w_in, nsa_q_norm, nsa_k_norm, cmp_pos_k, cmp_pos_v, cmp_w1_k, cmp_w2_k,
              cmp_w1_v, cmp_w2_v, gla_w_gate2, gla_b_gate, gla_out_norm, w_out, ffn_norm, w_up,
              conv_w, conv_b, w_down):
    offs = [int(o) for o in np.cumsum(IN_SPLITS)[:-1]]
    h = x
    for l in range(DEPTH):
        z = rms_norm(h, attn_norm[l]) @ w_in[l]
        nq, nkv, ngate, gq, gk, gv, glr, gout = jnp.split(z, offs, axis=-1)
        o_nsa = nsa_mixer(nq, nkv, ngate, nsa_q_norm[l], nsa_k_norm[l], cmp_pos_k[l], cmp_pos_v[l],
                          cmp_w1_k[l], cmp_w2_k[l], cmp_w1_v[l], cmp_w2_v[l])
        o_gla = gla_mixer(gq, gk, gv, glr, gout, gla_w_gate2[l], gla_b_gate[l], gla_out_norm[l])
        h = h + jnp.concatenate([o_nsa, o_gla], axis=-1) @ w_out[l]
        u = causal_dwconv(rms_norm(h, ffn_norm[l]) @ w_up[l], conv_w[l], conv_b[l])
        gate, up = jnp.split(u, 2, axis=-1)
        h = h + (jax.nn.silu(gate) * up) @ w_down[l]
    return h
```

```python
import functools

import numpy as np
import jax
import jax.numpy as jnp
from jax import lax
from jax.experimental import pallas as pl
from jax.experimental.pallas import tpu as pltpu

F32 = jnp.float32
BF16 = jnp.bfloat16

NSA_HEADS = 8
NSA_KV_GROUPS = 2
NSA_HEAD_DIM = 64
NSA_HPG = NSA_HEADS // NSA_KV_GROUPS
CMP_BLOCK = 32
CMP_STRIDE = 16
CMP_HIDDEN = 256
SLC_BLOCK = 64
SLC_TOPK = 8
WINDOW = 512
FORCE_SCORE = 1e4
GLA_HEADS = 4
GLA_DK = 64
GLA_DV = 128
GLA_GATE_RANK = 16
GLA_TAU = 16.0
GLA_CHUNK = 64
ROPE_THETA = 500000.0
ROT_DIM = NSA_HEAD_DIM // 4
ROT_HALF = ROT_DIM // 2
D_FF = 2816
CONV_W = 3
EPS = 1e-6
NEG = -1e30

NSA_Q_W = NSA_HEADS * NSA_HEAD_DIM
NSA_KV_W = NSA_KV_GROUPS * NSA_HEAD_DIM
GLA_QK_W = GLA_HEADS * GLA_DK
GLA_V_W = GLA_HEADS * GLA_DV
IN_SPLITS = [NSA_Q_W, 6 * NSA_KV_W, 3 * NSA_HEADS, GLA_QK_W, GLA_QK_W, GLA_V_W, GLA_GATE_RANK, GLA_V_W]

LANES = 128
SUBLANES = 8
GATE_ROWS = 32
VMEM_LIMIT = 56 * 1024 * 1024


def _dot(a, b):
    return jnp.dot(a, b, preferred_element_type=F32)


def _dot_nt(a, b):
    return lax.dot_general(a, b, (((1,), (1,)), ((), ())), preferred_element_type=F32)


def _split_dot(m_bf16, x_f32):
    hi = x_f32.astype(BF16)
    lo = (x_f32 - hi.astype(F32)).astype(BF16)
    return _dot(m_bf16, hi) + _dot(m_bf16, lo)


def _split_dot_r(x_f32, m_bf16):
    hi = x_f32.astype(BF16)
    lo = (x_f32 - hi.astype(F32)).astype(BF16)
    return _dot(hi, m_bf16) + _dot(lo, m_bf16)


def _rope_lanes(y, c, a, b):
    return y * c + pltpu.roll(y, LANES - ROT_HALF, 1) * a + pltpu.roll(y, ROT_HALF, 1) * b


def _sigmoid(z):
    return 1.0 / (1.0 + jnp.exp(-z))


def _in_proj_kernel(x_ref, g_ref, wf_ref, wt_ref, qg_ref, kgs_ref, kgw_ref, bd_ref,
                    cosT_ref, sinT_ref, rc_ref, ra_ref, rb_ref, wg2_ref, bg_ref,
                    qT_ref, vsT_ref, vwT_ref, gateT_ref, ks_ref, kw_ref, kvc_ref,
                    gq_ref, gk_ref, la_ref, gv_ref, gout_ref):
    x = x_ref[...]
    ms = jnp.mean(x * x, axis=-1, keepdims=True)
    xn = (x * lax.rsqrt(ms + EPS) * g_ref[...]).astype(BF16)

    zf = _dot_nt(wf_ref[...], xn)
    cosT = cosT_ref[...]
    sinT = sinT_ref[...]
    qg = qg_ref[...]
    scale = NSA_HEAD_DIM ** -0.5
    for h in range(NSA_HEADS):
        blk = zf[h * NSA_HEAD_DIM:(h + 1) * NSA_HEAD_DIM]
        hms = jnp.mean(blk * blk, axis=0, keepdims=True)
        y = blk * lax.rsqrt(hms + EPS) * qg
        y1 = y[0:ROT_HALF]
        y2 = y[ROT_HALF:ROT_DIM]
        r = jnp.concatenate([y1 * cosT - y2 * sinT, y2 * cosT + y1 * sinT, y[ROT_DIM:]], axis=0)
        qT_ref[0, h * NSA_HEAD_DIM:(h + 1) * NSA_HEAD_DIM, :] = (r * scale).astype(BF16)
    o = NSA_Q_W
    vsT_ref[0] = zf[o:o + NSA_KV_W].astype(BF16)
    vwT_ref[0] = zf[o + NSA_KV_W:o + 2 * NSA_KV_W].astype(BF16)
    gateT_ref[0] = _sigmoid(zf[o + 2 * NSA_KV_W:o + 2 * NSA_KV_W + GATE_ROWS])

    bd = bd_ref[...]
    rc, ra, rb = rc_ref[...], ra_ref[...], rb_ref[...]

    def seg(a, b):
        return _dot(xn, wt_ref[:, a:b])

    def knorm(z, gain):
        gms = _split_dot_r(z * z, bd)
        return _rope_lanes(z * lax.rsqrt(gms + EPS) * gain, rc, ra, rb).astype(BF16)

    c = 0
    ks_ref[...] = knorm(seg(c, c + NSA_KV_W), kgs_ref[...]); c += NSA_KV_W
    kw_ref[...] = knorm(seg(c, c + NSA_KV_W), kgw_ref[...]); c += NSA_KV_W
    kvc_ref[...] = seg(c, c + 2 * NSA_KV_W).astype(BF16); c += 2 * NSA_KV_W
    gq_ref[...] = seg(c, c + GLA_QK_W); c += GLA_QK_W
    gk_ref[...] = seg(c, c + GLA_QK_W); c += GLA_QK_W
    gv_ref[...] = seg(c, c + GLA_V_W).astype(BF16); c += GLA_V_W
    gout_ref[...] = seg(c, c + GLA_V_W).astype(BF16); c += GLA_V_W
    glr = seg(c, c + LANES).astype(BF16)
    pre = _dot(glr, wg2_ref[...]) + bg_ref[...]
    la_ref[...] = (jnp.minimum(pre, 0.0) - jnp.log1p(jnp.exp(-jnp.abs(pre)))) * (1.0 / GLA_TAU)


def _gelu_tanh(x):
    return 0.5 * x * (1.0 + jnp.tanh(0.7978845608028654 * (x + 0.044715 * (x * x * x))))


def _compress_kernel(r_ref, w1k_ref, w2k_ref, w1v_ref, w2v_ref, pek_ref, pev_ref, kg_ref,
                     cc_ref, ca_ref, cb_ref, kcmp_ref, vcT_ref, *, nc):
    half = CMP_STRIDE * NSA_HEAD_DIM
    for j in range(2 * NSA_KV_GROUPS):
        is_k = j < NSA_KV_GROUPS
        g = j % NSA_KV_GROUPS
        w1 = w1k_ref if is_k else w1v_ref
        w2 = w2k_ref if is_k else w2v_ref
        pe = pek_ref if is_k else pev_ref
        r = r_ref[0, j]
        p1 = _dot(r, w1[0:half, :])
        p2 = _dot(r, w1[half:2 * half, :])
        cst = _dot(pe[...], w1[...])[0:1]
        hid = p1 + pltpu.roll(p2, nc - 1, 0) + cst
        out = _dot(_gelu_tanh(hid).astype(BF16), w2[...])
        if is_k:
            kms = jnp.sum(out * out, axis=-1, keepdims=True) * (1.0 / NSA_HEAD_DIM)
            y = out * lax.rsqrt(kms + EPS) * kg_ref[...]
            kcmp_ref[0, g] = _rope_lanes(y, cc_ref[...], ca_ref[...], cb_ref[...]).astype(BF16)
        else:
            vcT_ref[0, g] = out.T[0:NSA_HEAD_DIM].astype(BF16)


def _nsa_kernel(qT_ref, kcmp_ref, vcT_ref, ovl_ref, eall_ref, ks_ref, vsT_ref, kw_ref, vwT_ref,
                gate_ref, o_ref, qaug_ref, acc_ref, m_ref, l_ref, out_ref,
                *, tq_size, nc, n_cmp, ns, top_k):
    TQ = tq_size
    TK = tq_size
    D = NSA_HEAD_DIM
    g = pl.program_id(1)
    i = pl.program_id(2)
    t0 = i * TQ
    tq = t0 + lax.broadcasted_iota(jnp.int32, (1, TQ), 1)

    def gate_row(h, c):
        return gate_ref[0, pl.ds((g * NSA_HPG + h) * 3 + c, 1), :]

    zeros64 = jnp.zeros((D, TQ), BF16)
    for h in range(NSA_HPG):
        qh = qT_ref[0, h * D:(h + 1) * D, :]
        qaug_ref[h, 0:D, :] = jnp.where(g == 0, qh, zeros64)
        qaug_ref[h, D:2 * D, :] = jnp.where(g == 1, qh, zeros64)

    kc = kcmp_ref[0, 0]
    vcT = vcT_ref[0, 0]
    n_idx = lax.broadcasted_iota(jnp.int32, (nc, TQ), 0)
    cmask = (n_idx * CMP_STRIDE + (CMP_BLOCK - 1) <= tq) & (n_idx < n_cmp)
    psum = jnp.zeros((nc, TQ), F32)
    for h in range(NSA_HPG):
        q128 = jnp.concatenate([qT_ref[0, h * D:(h + 1) * D, :], zeros64], axis=0)
        s = jnp.where(cmask, _dot(kc, q128), NEG)
        m = jnp.max(s, axis=0, keepdims=True)
        e = jnp.where(cmask, jnp.exp(s - m), 0.0)
        l = jnp.sum(e, axis=0, keepdims=True)
        p = e * jnp.where(l > 0.0, 1.0 / l, 0.0)
        psum = psum + p
        out_ref[h] = gate_row(h, 0) * _dot(vcT, p.astype(BF16))

    p_slc = _split_dot(ovl_ref[...], psum)
    j_idx = lax.broadcasted_iota(jnp.int32, (ns, TQ), 0)
    cur = tq // SLC_BLOCK
    forced = (j_idx == 0) | (j_idx == cur) | (j_idx == cur - 1)
    score = jnp.where(j_idx <= cur, jnp.where(forced, FORCE_SCORE, p_slc), -jnp.inf)
    rank = jnp.zeros((ns, TQ), F32)
    for a in range(ns):
        sa = score[a:a + 1, :]
        beats = (sa > score) | ((sa == score) & (j_idx > a))
        rank = rank + beats.astype(F32)
    selbias = jnp.where(rank < top_k, 0.0, NEG).astype(BF16)
    pad = jnp.zeros((LANES - ns, TQ), BF16)
    sb = jnp.concatenate([selbias, pad], axis=0)
    for h in range(NSA_HPG):
        qaug_ref[h, 2 * D:2 * D + LANES, :] = sb

    row = lax.broadcasted_iota(jnp.int32, (TK, TQ), 0)
    col = lax.broadcasted_iota(jnp.int32, (TK, TQ), 1)

    def flash_step(h, s, vT, first):
        if first:
            m_new = jnp.max(s, axis=0, keepdims=True)
            p = jnp.exp(s - m_new)
            l_ref[h] = jnp.sum(p, axis=0, keepdims=True)
            acc_ref[h] = _dot(vT, p.astype(BF16))
        else:
            m_old = m_ref[h]
            m_new = jnp.maximum(m_old, jnp.max(s, axis=0, keepdims=True))
            alpha = jnp.exp(m_old - m_new)
            p = jnp.exp(s - m_new)
            l_ref[h] = alpha * l_ref[h] + jnp.sum(p, axis=0, keepdims=True)
            acc_ref[h] = alpha * acc_ref[h] + _dot(vT, p.astype(BF16))
        m_ref[h] = m_new

    def sel_tile(kv, diag):
        k0 = pl.multiple_of(kv * TK, TK)
        kaug = jnp.concatenate([ks_ref[0, pl.ds(k0, TK), :], eall_ref[pl.ds(k0, TK), :]], axis=1)
        vT = vsT_ref[0, 0, :, pl.ds(k0, TK)]
        for h in range(NSA_HPG):
            s = _dot(kaug, qaug_ref[h])
            if diag:
                s = jnp.where(row <= col, s, NEG)
            flash_step(h, s, vT, diag)

    sel_tile(i, True)

    def sel_body(kv, carry):
        sel_tile(kv, False)
        return carry

    lax.fori_loop(0, i, sel_body, 0)
    for h in range(NSA_HPG):
        out_ref[h] = out_ref[h] + gate_row(h, 1) * (acc_ref[h] * (1.0 / l_ref[h]))

    n_back = WINDOW // TK

    def win_tile(kv, mode):
        k0 = pl.multiple_of(kv * TK, TK)
        kt = kw_ref[0, pl.ds(k0, TK), :]
        vT = vwT_ref[0, 0, :, pl.ds(k0, TK)]
        for h in range(NSA_HPG):
            s = _dot(kt, qaug_ref[h, 0:2 * D, :])
            if mode == "diag":
                s = jnp.where(row <= col, s, NEG)
            elif mode == "tail":
                s = jnp.where(row > col, s, NEG)
            flash_step(h, s, vT, mode == "diag")

    win_tile(i, "diag")
    for d in range(1, n_back + 1):
        @pl.when(i >= d)
        def _(d=d):
            win_tile(i - d, "tail" if d == n_back else "full")

    outs = []
    for h in range(NSA_HPG):
        outs.append(out_ref[h] + gate_row(h, 2) * (acc_ref[h] * (1.0 / l_ref[h])))
    o_ref[0] = jnp.concatenate(outs, axis=0).T.astype(BF16)


def _gla_kernel(gq_ref, gk_ref, la_ref, gv_ref, gout_ref, gain_ref, o_ref, st_ref, *, tg):
    L = GLA_CHUNK

    @pl.when(pl.program_id(1) == 0)
    def _():
        st_ref[...] = jnp.zeros_like(st_ref)

    la = la_ref[0]
    row = lax.broadcasted_iota(jnp.int32, (tg, tg), 0)
    col = lax.broadcasted_iota(jnp.int32, (tg, tg), 1)
    same = (row // L) == (col // L)
    tril = same & (col <= row)
    b = _split_dot(tril.astype(BF16), la)
    bl = _split_dot(same.astype(BF16), la)
    gk = gk_ref[0]
    qt = (gq_ref[0] * (GLA_DK ** -0.5) * jnp.exp(b))
    kt = (gk * jnp.exp(-b)).astype(BF16)
    kd = (gk * jnp.exp(bl - b)).astype(BF16)
    dec = jnp.exp(bl)
    lane = lax.broadcasted_iota(jnp.int32, (1, GLA_QK_W), 1)
    gain = gain_ref[...]
    for h in range(GLA_HEADS):
        qth = jnp.where((lane // GLA_DK) == h, qt, 0.0).astype(BF16)
        a = jnp.where(tril, _dot_nt(qth, kt), 0.0)
        vh = gv_ref[0, :, h * GLA_DV:(h + 1) * GLA_DV]
        o_intra = _dot(a.astype(BF16), vh)
        pieces = []
        for c in range(tg // L):
            rows = slice(c * L, (c + 1) * L)
            st = st_ref[h]
            pieces.append(o_intra[rows] + _dot_nt(qth[rows], st.astype(BF16)))
            vT = vh[rows].astype(F32).T.astype(BF16)
            st_ref[h] = st * dec[c * L:c * L + 1] + _dot(vT, kd[rows])
        o = jnp.concatenate(pieces, axis=0)
        oms = jnp.mean(o * o, axis=-1, keepdims=True)
        y = o * lax.rsqrt(oms + EPS) * gain
        go = gout_ref[0, :, h * GLA_DV:(h + 1) * GLA_DV].astype(F32)
        o_ref[0, :, h * GLA_DV:(h + 1) * GLA_DV] = (y * (go * _sigmoid(go))).astype(BF16)


def _out_proj_kernel(x_ref, on_ref, og_ref, wo_ref, h_ref):
    h_ref[...] = (x_ref[...] + _dot(on_ref[...], wo_ref[0:NSA_Q_W, :])
                  + _dot(og_ref[...], wo_ref[NSA_Q_W:NSA_Q_W + GLA_V_W, :]))


def _ffn_kernel(h_ref, halo_ref, g_ref, wup_ref, cw_ref, cb_ref, wdn_ref, o_ref, *, tm, tf, tiles_per_seq):
    h = h_ref[...]
    hh = jnp.concatenate([halo_ref[...], h], axis=0)
    ms = jnp.mean(hh * hh, axis=-1, keepdims=True)
    hn = (hh * lax.rsqrt(ms + EPS) * g_ref[...]).astype(BF16)
    seq_start = (pl.program_id(0) % tiles_per_seq) == 0
    ridx = lax.broadcasted_iota(jnp.int32, (tm + SUBLANES, 1), 0)
    keep = jnp.where((ridx < SUBLANES) & seq_start, 0.0, 1.0)

    def conv(col0):
        u = _dot(hn, wup_ref[:, col0:col0 + tf]) * keep
        w = cw_ref[:, col0:col0 + tf]
        y = (u * w[2:3] + pltpu.roll(u, 1, 0) * w[1:2] + pltpu.roll(u, 2, 0) * w[0:1])
        return y[SUBLANES:] + cb_ref[:, col0:col0 + tf]

    acc = h
    for c in range(D_FF // tf):
        gate = conv(c * tf)
        up = conv(D_FF + c * tf)
        act = (gate * _sigmoid(gate) * up).astype(BF16)
        acc = acc + _dot(act, wdn_ref[c * tf:(c + 1) * tf, :])
    o_ref[...] = acc


def _rope_tables(pos):
    inv = ROPE_THETA ** (-np.arange(ROT_HALF, dtype=np.float32) / ROT_HALF)
    ang = pos.astype(np.float32)[:, None] * inv[None, :]
    cos, sin = np.cos(ang), np.sin(ang)
    n = len(pos)
    c = np.ones((n, LANES), np.float32)
    a = np.zeros((n, LANES), np.float32)
    b = np.zeros((n, LANES), np.float32)
    for base in range(0, LANES, NSA_HEAD_DIM):
        c[:, base:base + ROT_HALF] = cos
        c[:, base + ROT_HALF:base + ROT_DIM] = cos
        a[:, base:base + ROT_HALF] = -sin
        b[:, base + ROT_HALF:base + ROT_DIM] = sin
    return cos, sin, c, a, b


def _const_spec(shape):
    nd = len(shape)
    return pl.BlockSpec(shape, lambda *_: (0,) * nd)


def kernel(x, attn_norm, w_in, nsa_q_norm, nsa_k_norm, cmp_pos_k, cmp_pos_v, cmp_w1_k, cmp_w2_k,
           cmp_w1_v, cmp_w2_v, gla_w_gate2, gla_b_gate, gla_out_norm, w_out, ffn_norm, w_up,
           conv_w, conv_b, w_down):
    B, S, DM = x.shape
    depth = attn_norm.shape[0]
    h = x
    for l in range(depth):
        h = _layer(h, attn_norm[l], w_in[l], nsa_q_norm[l], nsa_k_norm[l], cmp_pos_k[l], cmp_pos_v[l],
                   cmp_w1_k[l], cmp_w2_k[l], cmp_w1_v[l], cmp_w2_v[l], gla_w_gate2[l], gla_b_gate[l],
                   gla_out_norm[l], w_out[l], ffn_norm[l], w_up[l], conv_w[l], conv_b[l], w_down[l])
    return h


def _layer(x, attn_norm, w_in, q_gain, k_gains, pe_k, pe_v, w1_k, w2_k, w1_v, w2_v, w_gate2, b_gate,
           out_gain, w_out, ffn_norm, w_up, conv_w, conv_b, w_down):
    B, S, DM = x.shape
    G, D = NSA_KV_GROUPS, NSA_HEAD_DIM
    tm = min(512, S)
    tq = min(256, S)
    tg = min(256, S)
    n_s = S // tm
    nc = S // CMP_STRIDE
    n_cmp = (S - CMP_BLOCK) // CMP_STRIDE + 1
    ns = S // SLC_BLOCK
    top_k = min(SLC_TOPK, ns)
    assert S % tm == 0 and S % tq == 0 and WINDOW % tq == 0 and ns <= LANES and nc % SUBLANES == 0

    offs = np.cumsum([0] + IN_SPLITS)
    nq0, nkv0, ngate0, gq0, gk0, gv0, glr0, gout0 = [int(o) for o in offs[:-1]]
    kvw = NSA_KV_W
    cols = lambda a, n: w_in[:, a:a + n]
    w_feat = jnp.concatenate([cols(nq0, NSA_Q_W), cols(nkv0 + 3 * kvw, kvw), cols(nkv0 + 5 * kvw, kvw),
                              cols(ngate0, 3 * NSA_HEADS),
                              jnp.zeros((DM, GATE_ROWS - 3 * NSA_HEADS), w_in.dtype)], axis=1)
    wf = w_feat.T.astype(BF16)
    w_tok = jnp.concatenate([cols(nkv0 + 2 * kvw, kvw), cols(nkv0 + 4 * kvw, kvw),
                             cols(nkv0, kvw), cols(nkv0 + kvw, kvw),
                             cols(gq0, GLA_QK_W), cols(gk0, GLA_QK_W), cols(gv0, GLA_V_W),
                             cols(gout0, GLA_V_W), cols(glr0, GLA_GATE_RANK),
                             jnp.zeros((DM, LANES - GLA_GATE_RANK), w_in.dtype)], axis=1)
    wt = w_tok.astype(BF16)
    nf, nt = wf.shape[0], wt.shape[1]
    wg2 = jnp.concatenate([w_gate2, jnp.zeros((LANES - GLA_GATE_RANK, GLA_QK_W), w_gate2.dtype)],
                          axis=0).astype(BF16)
    cos, sin, rc, ra, rb = _rope_tables(np.arange(S))
    cosT, sinT = jnp.asarray(cos.T), jnp.asarray(sin.T)
    cmp_end = np.arange(nc) * CMP_STRIDE + CMP_BLOCK - 1
    _, _, cc, ca, cb = _rope_tables(cmp_end)
    bd = np.kron(np.eye(LANES // D, dtype=np.float32), np.full((D, D), 1.0 / D, np.float32))
    tile2 = lambda v: jnp.tile(v.reshape(1, D), (1, LANES // D))

    x2 = x.reshape(B * S, DM)
    tok = lambda w: pl.BlockSpec((tm, w), lambda s, b: (b * n_s + s, 0))
    feat = lambda r: pl.BlockSpec((1, r, tm), lambda s, b: (b, 0, s))
    stab = lambda r, w: pl.BlockSpec((r, w), lambda s, b: (0, s)) if r == ROT_HALF else \
        pl.BlockSpec((r, w), lambda s, b: (s, 0))
    outs = pl.pallas_call(
        _in_proj_kernel,
        grid=(n_s, B),
        in_specs=[tok(DM), _const_spec((1, DM)), _const_spec((nf, DM)), _const_spec((DM, nt)),
                  _const_spec((D, 1)), _const_spec((1, LANES)), _const_spec((1, LANES)),
                  _const_spec((LANES, LANES)),
                  stab(ROT_HALF, tm), stab(ROT_HALF, tm), stab(tm, LANES), stab(tm, LANES), stab(tm, LANES),
                  _const_spec((LANES, GLA_QK_W)), _const_spec((1, GLA_QK_W))],
        out_specs=[feat(NSA_Q_W), feat(kvw), feat(kvw), feat(GATE_ROWS),
                   tok(kvw), tok(kvw), tok(2 * kvw), tok(GLA_QK_W), tok(GLA_QK_W), tok(GLA_QK_W),
                   tok(GLA_V_W), tok(GLA_V_W)],
        out_shape=[jax.ShapeDtypeStruct((B, NSA_Q_W, S), BF16), jax.ShapeDtypeStruct((B, kvw, S), BF16),
                   jax.ShapeDtypeStruct((B, kvw, S), BF16), jax.ShapeDtypeStruct((B, GATE_ROWS, S), F32),
                   jax.ShapeDtypeStruct((B * S, kvw), BF16), jax.ShapeDtypeStruct((B * S, kvw), BF16),
                   jax.ShapeDtypeStruct((B * S, 2 * kvw), BF16),
                   jax.ShapeDtypeStruct((B * S, GLA_QK_W), F32), jax.ShapeDtypeStruct((B * S, GLA_QK_W), F32),
                   jax.ShapeDtypeStruct((B * S, GLA_QK_W), F32),
                   jax.ShapeDtypeStruct((B * S, GLA_V_W), BF16), jax.ShapeDtypeStruct((B * S, GLA_V_W), BF16)],
        compiler_params=pltpu.CompilerParams(dimension_semantics=("arbitrary", "arbitrary"),
                                             vmem_limit_bytes=VMEM_LIMIT),
        name="in_proj",
    )(x2, attn_norm.reshape(1, DM), wf, wt, q_gain.reshape(D, 1), tile2(k_gains[1]), tile2(k_gains[2]),
      jnp.asarray(bd, BF16), cosT, sinT, jnp.asarray(rc), jnp.asarray(ra), jnp.asarray(rb),
      wg2, b_gate.reshape(1, GLA_QK_W))
    qT, vsT, vwT, gateT, ks, kw, kvc, gq, gk, la, gv, gout = outs

    r = kvc.reshape(B, nc, CMP_STRIDE, 2 * G, D).transpose(0, 3, 1, 2, 4).reshape(B, 2 * G, nc, CMP_STRIDE * D)
    pad_w2 = lambda w: jnp.concatenate([w, jnp.zeros((CMP_HIDDEN, LANES - D), w.dtype)], axis=1).astype(BF16)
    pe_rows = lambda p: jnp.tile(p.reshape(1, CMP_BLOCK * D), (SUBLANES, 1)).astype(BF16)
    cin = CMP_BLOCK * D
    kcmp, vcT = pl.pallas_call(
        functools.partial(_compress_kernel, nc=nc),
        grid=(B,),
        in_specs=[pl.BlockSpec((1, 2 * G, nc, CMP_STRIDE * D), lambda b: (b, 0, 0, 0)),
                  _const_spec((cin, CMP_HIDDEN)), _const_spec((CMP_HIDDEN, LANES)),
                  _const_spec((cin, CMP_HIDDEN)), _const_spec((CMP_HIDDEN, LANES)),
                  _const_spec((SUBLANES, cin)), _const_spec((SUBLANES, cin)), _const_spec((1, LANES)),
                  _const_spec((nc, LANES)), _const_spec((nc, LANES)), _const_spec((nc, LANES))],
        out_specs=[pl.BlockSpec((1, G, nc, LANES), lambda b: (b, 0, 0, 0)),
                   pl.BlockSpec((1, G, D, nc), lambda b: (b, 0, 0, 0))],
        out_shape=[jax.ShapeDtypeStruct((B, G, nc, LANES), BF16), jax.ShapeDtypeStruct((B, G, D, nc), BF16)],
        compiler_params=pltpu.CompilerParams(dimension_semantics=("arbitrary",), vmem_limit_bytes=VMEM_LIMIT),
        name="compress",
    )(r, w1_k.astype(BF16), pad_w2(w2_k), w1_v.astype(BF16), pad_w2(w2_v), pe_rows(pe_k), pe_rows(pe_v),
      tile2(k_gains[0]), jnp.asarray(cc), jnp.asarray(ca), jnp.asarray(cb))

    ci = np.arange(nc)[None, :]
    sj = np.arange(ns)[:, None]
    ovl = ((ci * CMP_STRIDE < (sj + 1) * SLC_BLOCK) & (ci * CMP_STRIDE + CMP_BLOCK > sj * SLC_BLOCK)
           & (ci < n_cmp)).astype(np.float32)
    eall = (np.arange(S)[:, None] // SLC_BLOCK == np.arange(LANES)[None, :]).astype(np.float32)
    n_q = S // tq
    o_nsa = pl.pallas_call(
        functools.partial(_nsa_kernel, tq_size=tq, nc=nc, n_cmp=n_cmp, ns=ns, top_k=top_k),
        grid=(B, G, n_q),
        in_specs=[pl.BlockSpec((1, NSA_HPG * D, tq), lambda b, g, i: (b, g, i)),
                  pl.BlockSpec((1, 1, nc, LANES), lambda b, g, i: (b, g, 0, 0)),
                  pl.BlockSpec((1, 1, D, nc), lambda b, g, i: (b, g, 0, 0)),
                  _const_spec((ns, nc)), _const_spec((S, LANES)),
                  pl.BlockSpec((1, S, kvw), lambda b, g, i: (b, 0, 0)),
                  pl.BlockSpec((1, 1, D, S), lambda b, g, i: (b, g, 0, 0)),
                  pl.BlockSpec((1, S, kvw), lambda b, g, i: (b, 0, 0)),
                  pl.BlockSpec((1, 1, D, S), lambda b, g, i: (b, g, 0, 0)),
                  pl.BlockSpec((1, GATE_ROWS, tq), lambda b, g, i: (b, 0, i))],
        out_specs=pl.BlockSpec((1, tq, NSA_HPG * D), lambda b, g, i: (b, i, g)),
        out_shape=jax.ShapeDtypeStruct((B, S, NSA_Q_W), BF16),
        scratch_shapes=[pltpu.VMEM((NSA_HPG, 2 * D + LANES, tq), BF16),
                        pltpu.VMEM((NSA_HPG, D, tq), F32), pltpu.VMEM((NSA_HPG, 1, tq), F32),
                        pltpu.VMEM((NSA_HPG, 1, tq), F32), pltpu.VMEM((NSA_HPG, D, tq), F32)],
        compiler_params=pltpu.CompilerParams(dimension_semantics=("arbitrary", "arbitrary", "arbitrary"),
                                             vmem_limit_bytes=VMEM_LIMIT),
        name="nsa_attn",
    )(qT, kcmp, vcT, jnp.asarray(ovl, BF16), jnp.asarray(eall, BF16), ks.reshape(B, S, kvw),
      vsT.reshape(B, G, D, S), kw.reshape(B, S, kvw), vwT.reshape(B, G, D, S), gateT)

    g3 = lambda a: a.reshape(B, S, a.shape[-1])
    gspec = lambda w: pl.BlockSpec((1, tg, w), lambda b, t: (b, t, 0))
    o_gla = pl.pallas_call(
        functools.partial(_gla_kernel, tg=tg),
        grid=(B, S // tg),
        in_specs=[gspec(GLA_QK_W), gspec(GLA_QK_W), gspec(GLA_QK_W), gspec(GLA_V_W), gspec(GLA_V_W),
                  _const_spec((1, GLA_DV))],
        out_specs=gspec(GLA_V_W),
        out_shape=jax.ShapeDtypeStruct((B, S, GLA_V_W), BF16),
        scratch_shapes=[pltpu.VMEM((GLA_HEADS, GLA_DV, GLA_QK_W), F32)],
        compiler_params=pltpu.CompilerParams(dimension_semantics=("arbitrary", "arbitrary"),
                                             vmem_limit_bytes=VMEM_LIMIT),
        name="gla",
    )(g3(gq), g3(gk), g3(la), g3(gv), g3(gout), out_gain.reshape(1, GLA_DV))

    n_t = (B * S) // tm
    row = lambda w: pl.BlockSpec((tm, w), lambda t: (t, 0))
    hmid = pl.pallas_call(
        _out_proj_kernel,
        grid=(n_t,),
        in_specs=[row(DM), row(NSA_Q_W), row(GLA_V_W), _const_spec((NSA_Q_W + GLA_V_W, DM))],
        out_specs=row(DM),
        out_shape=jax.ShapeDtypeStruct((B * S, DM), F32),
        compiler_params=pltpu.CompilerParams(dimension_semantics=("arbitrary",), vmem_limit_bytes=VMEM_LIMIT),
        name="out_proj",
    )(x2, o_nsa.reshape(B * S, NSA_Q_W), o_gla.reshape(B * S, GLA_V_W), w_out.astype(BF16))

    tf = 256
    assert D_FF % tf == 0
    halo_blocks = tm // SUBLANES
    out = pl.pallas_call(
        functools.partial(_ffn_kernel, tm=tm, tf=tf, tiles_per_seq=n_s),
        grid=(n_t,),
        in_specs=[row(DM),
                  pl.BlockSpec((SUBLANES, DM), lambda t: (jnp.maximum(t * halo_blocks - 1, 0), 0)),
                  _const_spec((1, DM)),
                  pl.BlockSpec((DM, 2 * D_FF), lambda t: (0, 0), pipeline_mode=pl.Buffered(1)),
                  _const_spec((CONV_W, 2 * D_FF)), _const_spec((1, 2 * D_FF)),
                  pl.BlockSpec((D_FF, DM), lambda t: (0, 0), pipeline_mode=pl.Buffered(1))],
        out_specs=row(DM),
        out_shape=jax.ShapeDtypeStruct((B * S, DM), F32),
        compiler_params=pltpu.CompilerParams(dimension_semantics=("arbitrary",), vmem_limit_bytes=VMEM_LIMIT),
        name="ffn",
    )(hmid, hmid, ffn_norm.reshape(1, DM), w_up.astype(BF16), conv_w, conv_b.reshape(1, 2 * D_FF),
      w_down.astype(BF16))
    return out.reshape(B, S, DM)
```

```python
import functools

import numpy as np
import jax
import jax.numpy as jnp
from jax import lax
from jax.experimental import pallas as pl
from jax.experimental.pallas import tpu as pltpu

F32 = jnp.float32
BF16 = jnp.bfloat16

NSA_HEADS = 8
NSA_KV_GROUPS = 2
NSA_HEAD_DIM = 64
NSA_HPG = NSA_HEADS // NSA_KV_GROUPS
CMP_BLOCK = 32
CMP_STRIDE = 16
CMP_HIDDEN = 256
SLC_BLOCK = 64
SLC_TOPK = 8
WINDOW = 512
FORCE_SCORE = 1e4
GLA_HEADS = 4
GLA_DK = 64
GLA_DV = 128
GLA_GATE_RANK = 16
GLA_TAU = 16.0
GLA_CHUNK = 64
ROPE_THETA = 500000.0
ROT_DIM = NSA_HEAD_DIM // 4
ROT_HALF = ROT_DIM // 2
D_FF = 2816
CONV_W = 3
EPS = 1e-6
NEG = -1e30
LOG2E = 1.4426950408889634

NSA_Q_W = NSA_HEADS * NSA_HEAD_DIM
NSA_KV_W = NSA_KV_GROUPS * NSA_HEAD_DIM
GLA_QK_W = GLA_HEADS * GLA_DK
GLA_V_W = GLA_HEADS * GLA_DV
IN_SPLITS = [NSA_Q_W, 6 * NSA_KV_W, 3 * NSA_HEADS, GLA_QK_W, GLA_QK_W, GLA_V_W, GLA_GATE_RANK, GLA_V_W]

LANES = 128
SUBLANES = 8
GATE_ROWS = 32
VMEM_LIMIT = 56 * 1024 * 1024


def _dot(a, b):
    return jnp.dot(a, b, preferred_element_type=F32)


def _dot_nt(a, b):
    return lax.dot_general(a, b, (((1,), (1,)), ((), ())), preferred_element_type=F32)


def _split_dot(m_bf16, x_f32):
    hi = x_f32.astype(BF16)
    lo = (x_f32 - hi.astype(F32)).astype(BF16)
    return _dot(m_bf16, hi) + _dot(m_bf16, lo)


def _split_dot_r(x_f32, m_bf16):
    hi = x_f32.astype(BF16)
    lo = (x_f32 - hi.astype(F32)).astype(BF16)
    return _dot(hi, m_bf16) + _dot(lo, m_bf16)


def _rope_lanes(y, c, a, b):
    return y * c + pltpu.roll(y, LANES - ROT_HALF, 1) * a + pltpu.roll(y, ROT_HALF, 1) * b


def _sigmoid(z):
    return 1.0 / (1.0 + jnp.exp(-z))


def _in_proj_kernel(x_ref, g_ref, wf_ref, wt_ref, qg_ref, kgs_ref, kgw_ref, bd_ref,
                    cosT_ref, sinT_ref, rc_ref, ra_ref, rb_ref, wg2_ref, bg_ref,
                    qT_ref, vsT_ref, vwT_ref, gateT_ref, ks_ref, kw_ref, kvc_ref,
                    gq_ref, gk_ref, la_ref, gv_ref, gout_ref):
    x = x_ref[...]
    ms = jnp.mean(x * x, axis=-1, keepdims=True)
    xn = (x * lax.rsqrt(ms + EPS) * g_ref[...]).astype(BF16)

    zf = _dot_nt(wf_ref[...], xn)
    cosT = cosT_ref[...]
    sinT = sinT_ref[...]
    qg = qg_ref[...]
    scale = NSA_HEAD_DIM ** -0.5 * LOG2E
    for h in range(NSA_HEADS):
        blk = zf[h * NSA_HEAD_DIM:(h + 1) * NSA_HEAD_DIM]
        hms = jnp.mean(blk * blk, axis=0, keepdims=True)
        y = blk * lax.rsqrt(hms + EPS) * qg
        y1 = y[0:ROT_HALF]
        y2 = y[ROT_HALF:ROT_DIM]
        r = jnp.concatenate([y1 * cosT - y2 * sinT, y2 * cosT + y1 * sinT, y[ROT_DIM:]], axis=0)
        qT_ref[0, h * NSA_HEAD_DIM:(h + 1) * NSA_HEAD_DIM, :] = (r * scale).astype(BF16)
    o = NSA_Q_W
    vsT_ref[0] = zf[o:o + NSA_KV_W].astype(BF16)
    vwT_ref[0] = zf[o + NSA_KV_W:o + 2 * NSA_KV_W].astype(BF16)
    gateT_ref[0] = _sigmoid(zf[o + 2 * NSA_KV_W:o + 2 * NSA_KV_W + GATE_ROWS])

    bd = bd_ref[...]
    rc, ra, rb = rc_ref[...], ra_ref[...], rb_ref[...]

    def seg(a, b):
        return _dot(xn, wt_ref[:, a:b])

    def knorm(z, gain):
        gms = _split_dot_r(z * z, bd)
        return _rope_lanes(z * lax.rsqrt(gms + EPS) * gain, rc, ra, rb).astype(BF16)

    c = 0
    ks_ref[...] = knorm(seg(c, c + NSA_KV_W), kgs_ref[...]); c += NSA_KV_W
    kw_ref[...] = knorm(seg(c, c + NSA_KV_W), kgw_ref[...]); c += NSA_KV_W
    kvc_ref[...] = seg(c, c + 2 * NSA_KV_W).astype(BF16); c += 2 * NSA_KV_W
    gq_ref[...] = seg(c, c + GLA_QK_W); c += GLA_QK_W
    gk_ref[...] = seg(c, c + GLA_QK_W); c += GLA_QK_W
    gv_ref[...] = seg(c, c + GLA_V_W).astype(BF16); c += GLA_V_W
    gout_ref[...] = seg(c, c + GLA_V_W).astype(BF16); c += GLA_V_W
    glr = seg(c, c + LANES).astype(BF16)
    pre = _dot(glr, wg2_ref[...]) + bg_ref[...]
    la_ref[...] = (jnp.minimum(pre, 0.0) - jnp.log1p(jnp.exp(-jnp.abs(pre)))) * (1.0 / GLA_TAU)


def _gelu_tanh(x):
    return 0.5 * x * (1.0 + jnp.tanh(0.7978845608028654 * (x + 0.044715 * (x * x * x))))


def _compress_kernel(r_ref, w1k_ref, w2k_ref, w1v_ref, w2v_ref, pek_ref, pev_ref, kg_ref,
                     cc_ref, ca_ref, cb_ref, kcmp_ref, vcT_ref, *, nc):
    half = CMP_STRIDE * NSA_HEAD_DIM
    for j in range(2 * NSA_KV_GROUPS):
        is_k = j < NSA_KV_GROUPS
        g = j % NSA_KV_GROUPS
        w1 = w1k_ref if is_k else w1v_ref
        w2 = w2k_ref if is_k else w2v_ref
        pe = pek_ref if is_k else pev_ref
        r = r_ref[0, j]
        p1 = _dot(r, w1[0:half, :])
        p2 = _dot(r, w1[half:2 * half, :])
        cst = _dot(pe[...], w1[...])[0:1]
        hid = p1 + pltpu.roll(p2, nc - 1, 0) + cst
        out = _dot(_gelu_tanh(hid).astype(BF16), w2[...])
        if is_k:
            kms = jnp.sum(out * out, axis=-1, keepdims=True) * (1.0 / NSA_HEAD_DIM)
            y = out * lax.rsqrt(kms + EPS) * kg_ref[...]
            kcmp_ref[0, g] = _rope_lanes(y, cc_ref[...], ca_ref[...], cb_ref[...]).astype(BF16)
        else:
            vcT_ref[0, g] = out.T[0:NSA_HEAD_DIM].astype(BF16)


def _nsa_kernel(qT_ref, kcmp_ref, vcT_ref, ovl_ref, eall_ref, ks_ref, vsT_ref, kw_ref, vwT_ref,
                gate_ref, o_ref, qall_ref, acc_ref, m_ref, out_ref,
                *, tq_size, nc, n_cmp, ns, top_k):
    TQ = tq_size
    TK = tq_size
    D = NSA_HEAD_DIM
    G = NSA_KV_GROUPS
    GW = NSA_HPG * TQ
    W = NSA_HEADS * TQ
    i = pl.program_id(1)
    t0 = i * TQ

    zeros = jnp.zeros((D, TQ), BF16)
    for gh in range(NSA_HEADS):
        g = gh // NSA_HPG
        cols = slice(gh * TQ, (gh + 1) * TQ)
        qall_ref[g * D:(g + 1) * D, cols] = qT_ref[0, gh * D:(gh + 1) * D, :]
        qall_ref[(1 - g) * D:(2 - g) * D, cols] = zeros

    def gate_vec(c):
        return jnp.concatenate([gate_ref[0, gh * 3 + c:gh * 3 + c + 1, :] for gh in range(NSA_HEADS)], axis=1)

    tq = t0 + lax.broadcasted_iota(jnp.int32, (1, TQ), 1)
    tq_g = t0 + (lax.broadcasted_iota(jnp.int32, (1, GW), 1) & (TQ - 1))
    n_idx = lax.broadcasted_iota(jnp.int32, (nc, GW), 0)
    cmask = (n_idx * CMP_STRIDE + (CMP_BLOCK - 1) <= tq_g) & (n_idx < n_cmp)
    j_idx = lax.broadcasted_iota(jnp.int32, (ns, TQ), 0)
    cur = tq // SLC_BLOCK
    forced = (j_idx == 0) | (j_idx == cur) | (j_idx == cur - 1)
    zpad = jnp.zeros((D, GW), BF16)
    for g in range(G):
        gcols = slice(g * GW, (g + 1) * GW)
        q128 = jnp.concatenate([qall_ref[g * D:(g + 1) * D, gcols], zpad], axis=0)
        s = jnp.where(cmask, _dot(kcmp_ref[0, g], q128), NEG)
        m = jnp.max(s, axis=0, keepdims=True)
        e = jnp.where(cmask, jnp.exp2(s - m), 0.0)
        l = jnp.sum(e, axis=0, keepdims=True)
        p = e * jnp.where(l > 0.0, 1.0 / l, 0.0)
        out_ref[:, gcols] = _dot(vcT_ref[0, g], p.astype(BF16))
        psum = p[:, 0:TQ]
        for h in range(1, NSA_HPG):
            psum = psum + p[:, h * TQ:(h + 1) * TQ]
        p_slc = _split_dot(ovl_ref[...], psum)
        score = jnp.where(j_idx <= cur, jnp.where(forced, FORCE_SCORE, p_slc), -jnp.inf)
        rank = jnp.zeros((ns, TQ), F32)
        for a in range(ns):
            sa = score[a:a + 1, :]
            beats = (sa > score) | ((sa == score) & (j_idx > a))
            rank = rank + beats.astype(F32)
        selbias = jnp.where(rank < top_k, 0.0, NEG).astype(BF16)
        sb = jnp.concatenate([selbias, jnp.zeros((LANES - ns, TQ), BF16)], axis=0)
        qall_ref[2 * D:2 * D + LANES, gcols] = jnp.concatenate([sb] * NSA_HPG, axis=1)
    out_ref[...] = out_ref[...] * gate_vec(0)

    row = lax.broadcasted_iota(jnp.int32, (TK, W), 0)
    qcol = lax.broadcasted_iota(jnp.int32, (TK, W), 1) & (TQ - 1)
    ones_rows = jnp.ones((2 * SUBLANES, TK), BF16)

    def flash_step(s, vT, first):
        smax = jnp.max(s, axis=0, keepdims=True)
        m_new = smax if first else jnp.maximum(m_ref[...], smax)
        p = jnp.exp2(s - m_new).astype(BF16)
        pv = jnp.concatenate(
            [_dot(jnp.concatenate([vT[g * D:(g + 1) * D], ones_rows], axis=0), p[:, g * GW:(g + 1) * GW])
             for g in range(G)], axis=1)
        if first:
            acc_ref[...] = pv
        else:
            acc_ref[...] = jnp.exp2(m_ref[...] - m_new) * acc_ref[...] + pv
        m_ref[...] = m_new

    def branch_out(c):
        return gate_vec(c) * (acc_ref[0:D, :] * (1.0 / acc_ref[D:D + 1, :]))

    def sel_tile(kv, diag):
        k0 = pl.multiple_of(kv * TK, TK)
        kaug = jnp.concatenate([ks_ref[0, pl.ds(k0, TK), :], eall_ref[pl.ds(k0, TK), :]], axis=1)
        s = _dot(kaug, qall_ref[...])
        if diag:
            s = jnp.where(row <= qcol, s, NEG)
        flash_step(s, vsT_ref[0, :, pl.ds(k0, TK)], diag)

    sel_tile(i, True)

    def sel_body(kv, carry):
        sel_tile(kv, False)
        return carry

    lax.fori_loop(0, i, sel_body, 0)
    out_ref[...] = out_ref[...] + branch_out(1)

    n_back = WINDOW // TK

    def win_tile(kv, mode):
        k0 = pl.multiple_of(kv * TK, TK)
        s = _dot(kw_ref[0, pl.ds(k0, TK), :], qall_ref[0:2 * D, :])
        if mode == "diag":
            s = jnp.where(row <= qcol, s, NEG)
        elif mode == "tail":
            s = jnp.where(row > qcol, s, NEG)
        flash_step(s, vwT_ref[0, :, pl.ds(k0, TK)], mode == "diag")

    win_tile(i, "diag")
    for d in range(1, n_back + 1):
        @pl.when(i >= d)
        def _(d=d):
            win_tile(i - d, "tail" if d == n_back else "full")

    out = out_ref[...] + branch_out(2)
    stacked = jnp.concatenate([out[:, gh * TQ:(gh + 1) * TQ] for gh in range(NSA_HEADS)], axis=0)
    o_ref[0] = stacked.T.astype(BF16)


def _gla_kernel(gq_ref, gk_ref, la_ref, gv_ref, gout_ref, gain_ref, o_ref, st_ref, *, tg):
    L = GLA_CHUNK

    @pl.when(pl.program_id(1) == 0)
    def _():
        st_ref[...] = jnp.zeros_like(st_ref)

    la = la_ref[0]
    row = lax.broadcasted_iota(jnp.int32, (tg, tg), 0)
    col = lax.broadcasted_iota(jnp.int32, (tg, tg), 1)
    same = (row // L) == (col // L)
    tril = same & (col <= row)
    b = _split_dot(tril.astype(BF16), la)
    bl = _split_dot(same.astype(BF16), la)
    gk = gk_ref[0]
    qt = (gq_ref[0] * (GLA_DK ** -0.5) * jnp.exp(b))
    kt = (gk * jnp.exp(-b)).astype(BF16)
    kd = (gk * jnp.exp(bl - b)).astype(BF16)
    dec = jnp.exp(bl)
    lane = lax.broadcasted_iota(jnp.int32, (1, GLA_QK_W), 1)
    gain = gain_ref[...]
    for h in range(GLA_HEADS):
        qth = jnp.where((lane // GLA_DK) == h, qt, 0.0).astype(BF16)
        a = jnp.where(tril, _dot_nt(qth, kt), 0.0)
        vh = gv_ref[0, :, h * GLA_DV:(h + 1) * GLA_DV]
        o_intra = _dot(a.astype(BF16), vh)
        pieces = []
        for c in range(tg // L):
            rows = slice(c * L, (c + 1) * L)
            st = st_ref[h]
            pieces.append(o_intra[rows] + _dot_nt(qth[rows], st.astype(BF16)))
            vT = vh[rows].astype(F32).T.astype(BF16)
            st_ref[h] = st * dec[c * L:c * L + 1] + _dot(vT, kd[rows])
        o = jnp.concatenate(pieces, axis=0)
        oms = jnp.mean(o * o, axis=-1, keepdims=True)
        y = o * lax.rsqrt(oms + EPS) * gain
        go = gout_ref[0, :, h * GLA_DV:(h + 1) * GLA_DV].astype(F32)
        o_ref[0, :, h * GLA_DV:(h + 1) * GLA_DV] = (y * (go * _sigmoid(go))).astype(BF16)


def _out_proj_kernel(x_ref, on_ref, og_ref, wo_ref, h_ref):
    h_ref[...] = (x_ref[...] + _dot(on_ref[...], wo_ref[0:NSA_Q_W, :])
                  + _dot(og_ref[...], wo_ref[NSA_Q_W:NSA_Q_W + GLA_V_W, :]))


def _ffn_kernel(h_ref, halo_ref, g_ref, wup_ref, cw_ref, cb_ref, wdn_ref, o_ref, *, tm, tf, tiles_per_seq):
    h = h_ref[...]
    hh = jnp.concatenate([halo_ref[...], h], axis=0)
    ms = jnp.mean(hh * hh, axis=-1, keepdims=True)
    hn = (hh * lax.rsqrt(ms + EPS) * g_ref[...]).astype(BF16)
    seq_start = (pl.program_id(0) % tiles_per_seq) == 0
    ridx = lax.broadcasted_iota(jnp.int32, (tm + SUBLANES, 1), 0)
    keep = jnp.where((ridx < SUBLANES) & seq_start, 0.0, 1.0)

    def conv(col0):
        u = _dot(hn, wup_ref[:, col0:col0 + tf]) * keep
        w = cw_ref[:, col0:col0 + tf]
        y = (u * w[2:3] + pltpu.roll(u, 1, 0) * w[1:2] + pltpu.roll(u, 2, 0) * w[0:1])
        return y[SUBLANES:] + cb_ref[:, col0:col0 + tf]

    acc = h
    for c in range(D_FF // tf):
        gate = conv(c * tf)
        up = conv(D_FF + c * tf)
        act = (gate * _sigmoid(gate) * up).astype(BF16)
        acc = acc + _dot(act, wdn_ref[c * tf:(c + 1) * tf, :])
    o_ref[...] = acc


def _rope_tables(pos):
    inv = ROPE_THETA ** (-np.arange(ROT_HALF, dtype=np.float32) / ROT_HALF)
    ang = pos.astype(np.float32)[:, None] * inv[None, :]
    cos, sin = np.cos(ang), np.sin(ang)
    n = len(pos)
    c = np.ones((n, LANES), np.float32)
    a = np.zeros((n, LANES), np.float32)
    b = np.zeros((n, LANES), np.float32)
    for base in range(0, LANES, NSA_HEAD_DIM):
        c[:, base:base + ROT_HALF] = cos
        c[:, base + ROT_HALF:base + ROT_DIM] = cos
        a[:, base:base + ROT_HALF] = -sin
        b[:, base + ROT_HALF:base + ROT_DIM] = sin
    return cos, sin, c, a, b


def _const_spec(shape):
    nd = len(shape)
    return pl.BlockSpec(shape, lambda *_: (0,) * nd)


def kernel(x, attn_norm, w_in, nsa_q_norm, nsa_k_norm, cmp_pos_k, cmp_pos_v, cmp_w1_k, cmp_w2_k,
           cmp_w1_v, cmp_w2_v, gla_w_gate2, gla_b_gate, gla_out_norm, w_out, ffn_norm, w_up,
           conv_w, conv_b, w_down):
    B, S, DM = x.shape
    depth = attn_norm.shape[0]
    h = x
    for l in range(depth):
        h = _layer(h, attn_norm[l], w_in[l], nsa_q_norm[l], nsa_k_norm[l], cmp_pos_k[l], cmp_pos_v[l],
                   cmp_w1_k[l], cmp_w2_k[l], cmp_w1_v[l], cmp_w2_v[l], gla_w_gate2[l], gla_b_gate[l],
                   gla_out_norm[l], w_out[l], ffn_norm[l], w_up[l], conv_w[l], conv_b[l], w_down[l])
    return h


def _layer(x, attn_norm, w_in, q_gain, k_gains, pe_k, pe_v, w1_k, w2_k, w1_v, w2_v, w_gate2, b_gate,
           out_gain, w_out, ffn_norm, w_up, conv_w, conv_b, w_down):
    B, S, DM = x.shape
    G, D = NSA_KV_GROUPS, NSA_HEAD_DIM
    tm = min(512, S)
    tq = min(256, S)
    tg = min(256, S)
    n_s = S // tm
    nc = S // CMP_STRIDE
    n_cmp = (S - CMP_BLOCK) // CMP_STRIDE + 1
    ns = S // SLC_BLOCK
    top_k = min(SLC_TOPK, ns)
    assert S % tm == 0 and S % tq == 0 and WINDOW % tq == 0 and ns <= LANES and nc % SUBLANES == 0

    offs = np.cumsum([0] + IN_SPLITS)
    nq0, nkv0, ngate0, gq0, gk0, gv0, glr0, gout0 = [int(o) for o in offs[:-1]]
    kvw = NSA_KV_W
    cols = lambda a, n: w_in[:, a:a + n]
    w_feat = jnp.concatenate([cols(nq0, NSA_Q_W), cols(nkv0 + 3 * kvw, kvw), cols(nkv0 + 5 * kvw, kvw),
                              cols(ngate0, 3 * NSA_HEADS),
                              jnp.zeros((DM, GATE_ROWS - 3 * NSA_HEADS), w_in.dtype)], axis=1)
    wf = w_feat.T.astype(BF16)
    w_tok = jnp.concatenate([cols(nkv0 + 2 * kvw, kvw), cols(nkv0 + 4 * kvw, kvw),
                             cols(nkv0, kvw), cols(nkv0 + kvw, kvw),
                             cols(gq0, GLA_QK_W), cols(gk0, GLA_QK_W), cols(gv0, GLA_V_W),
                             cols(gout0, GLA_V_W), cols(glr0, GLA_GATE_RANK),
                             jnp.zeros((DM, LANES - GLA_GATE_RANK), w_in.dtype)], axis=1)
    wt = w_tok.astype(BF16)
    nf, nt = wf.shape[0], wt.shape[1]
    wg2 = jnp.concatenate([w_gate2, jnp.zeros((LANES - GLA_GATE_RANK, GLA_QK_W), w_gate2.dtype)],
                          axis=0).astype(BF16)
    cos, sin, rc, ra, rb = _rope_tables(np.arange(S))
    cosT, sinT = jnp.asarray(cos.T), jnp.asarray(sin.T)
    cmp_end = np.arange(nc) * CMP_STRIDE + CMP_BLOCK - 1
    _, _, cc, ca, cb = _rope_tables(cmp_end)
    bd = np.kron(np.eye(LANES // D, dtype=np.float32), np.full((D, D), 1.0 / D, np.float32))
    tile2 = lambda v: jnp.tile(v.reshape(1, D), (1, LANES // D))

    x2 = x.reshape(B * S, DM)
    tok = lambda w: pl.BlockSpec((tm, w), lambda s, b: (b * n_s + s, 0))
    feat = lambda r: pl.BlockSpec((1, r, tm), lambda s, b: (b, 0, s))
    stab = lambda r, w: pl.BlockSpec((r, w), lambda s, b: (0, s)) if r == ROT_HALF else \
        pl.BlockSpec((r, w), lambda s, b: (s, 0))
    outs = pl.pallas_call(
        _in_proj_kernel,
        grid=(n_s, B),
        in_specs=[tok(DM), _const_spec((1, DM)), _const_spec((nf, DM)), _const_spec((DM, nt)),
                  _const_spec((D, 1)), _const_spec((1, LANES)), _const_spec((1, LANES)),
                  _const_spec((LANES, LANES)),
                  stab(ROT_HALF, tm), stab(ROT_HALF, tm), stab(tm, LANES), stab(tm, LANES), stab(tm, LANES),
                  _const_spec((LANES, GLA_QK_W)), _const_spec((1, GLA_QK_W))],
        out_specs=[feat(NSA_Q_W), feat(kvw), feat(kvw), feat(GATE_ROWS),
                   tok(kvw), tok(kvw), tok(2 * kvw), tok(GLA_QK_W), tok(GLA_QK_W), tok(GLA_QK_W),
                   tok(GLA_V_W), tok(GLA_V_W)],
        out_shape=[jax.ShapeDtypeStruct((B, NSA_Q_W, S), BF16), jax.ShapeDtypeStruct((B, kvw, S), BF16),
                   jax.ShapeDtypeStruct((B, kvw, S), BF16), jax.ShapeDtypeStruct((B, GATE_ROWS, S), F32),
                   jax.ShapeDtypeStruct((B * S, kvw), BF16), jax.ShapeDtypeStruct((B * S, kvw), BF16),
                   jax.ShapeDtypeStruct((B * S, 2 * kvw), BF16),
                   jax.ShapeDtypeStruct((B * S, GLA_QK_W), F32), jax.ShapeDtypeStruct((B * S, GLA_QK_W), F32),
                   jax.ShapeDtypeStruct((B * S, GLA_QK_W), F32),
                   jax.ShapeDtypeStruct((B * S, GLA_V_W), BF16), jax.ShapeDtypeStruct((B * S, GLA_V_W), BF16)],
        compiler_params=pltpu.CompilerParams(dimension_semantics=("arbitrary", "arbitrary"),
                                             vmem_limit_bytes=VMEM_LIMIT),
        name="in_proj",
    )(x2, attn_norm.reshape(1, DM), wf, wt, q_gain.reshape(D, 1), tile2(k_gains[1]), tile2(k_gains[2]),
      jnp.asarray(bd, BF16), cosT, sinT, jnp.asarray(rc), jnp.asarray(ra), jnp.asarray(rb),
      wg2, b_gate.reshape(1, GLA_QK_W))
    qT, vsT, vwT, gateT, ks, kw, kvc, gq, gk, la, gv, gout = outs

    r = kvc.reshape(B, nc, CMP_STRIDE, 2 * G, D).transpose(0, 3, 1, 2, 4).reshape(B, 2 * G, nc, CMP_STRIDE * D)
    pad_w2 = lambda w: jnp.concatenate([w, jnp.zeros((CMP_HIDDEN, LANES - D), w.dtype)], axis=1).astype(BF16)
    pe_rows = lambda p: jnp.tile(p.reshape(1, CMP_BLOCK * D), (SUBLANES, 1)).astype(BF16)
    cin = CMP_BLOCK * D
    kcmp, vcT = pl.pallas_call(
        functools.partial(_compress_kernel, nc=nc),
        grid=(B,),
        in_specs=[pl.BlockSpec((1, 2 * G, nc, CMP_STRIDE * D), lambda b: (b, 0, 0, 0)),
                  _const_spec((cin, CMP_HIDDEN)), _const_spec((CMP_HIDDEN, LANES)),
                  _const_spec((cin, CMP_HIDDEN)), _const_spec((CMP_HIDDEN, LANES)),
                  _const_spec((SUBLANES, cin)), _const_spec((SUBLANES, cin)), _const_spec((1, LANES)),
                  _const_spec((nc, LANES)), _const_spec((nc, LANES)), _const_spec((nc, LANES))],
        out_specs=[pl.BlockSpec((1, G, nc, LANES), lambda b: (b, 0, 0, 0)),
                   pl.BlockSpec((1, G, D, nc), lambda b: (b, 0, 0, 0))],
        out_shape=[jax.ShapeDtypeStruct((B, G, nc, LANES), BF16), jax.ShapeDtypeStruct((B, G, D, nc), BF16)],
        compiler_params=pltpu.CompilerParams(dimension_semantics=("arbitrary",), vmem_limit_bytes=VMEM_LIMIT),
        name="compress",
    )(r, w1_k.astype(BF16), pad_w2(w2_k), w1_v.astype(BF16), pad_w2(w2_v), pe_rows(pe_k), pe_rows(pe_v),
      tile2(k_gains[0]), jnp.asarray(cc), jnp.asarray(ca), jnp.asarray(cb))

    ci = np.arange(nc)[None, :]
    sj = np.arange(ns)[:, None]
    ovl = ((ci * CMP_STRIDE < (sj + 1) * SLC_BLOCK) & (ci * CMP_STRIDE + CMP_BLOCK > sj * SLC_BLOCK)
           & (ci < n_cmp)).astype(np.float32)
    eall = (np.arange(S)[:, None] // SLC_BLOCK == np.arange(LANES)[None, :]).astype(np.float32)
    n_q = S // tq
    o_nsa = pl.pallas_call(
        functools.partial(_nsa_kernel, tq_size=tq, nc=nc, n_cmp=n_cmp, ns=ns, top_k=top_k),
        grid=(B, n_q),
        in_specs=[pl.BlockSpec((1, NSA_Q_W, tq), lambda b, i: (b, 0, i)),
                  pl.BlockSpec((1, G, nc, LANES), lambda b, i: (b, 0, 0, 0)),
                  pl.BlockSpec((1, G, D, nc), lambda b, i: (b, 0, 0, 0)),
                  _const_spec((ns, nc)), _const_spec((S, LANES)),
                  pl.BlockSpec((1, S, kvw), lambda b, i: (b, 0, 0)),
                  pl.BlockSpec((1, kvw, S), lambda b, i: (b, 0, 0)),
                  pl.BlockSpec((1, S, kvw), lambda b, i: (b, 0, 0)),
                  pl.BlockSpec((1, kvw, S), lambda b, i: (b, 0, 0)),
                  pl.BlockSpec((1, GATE_ROWS, tq), lambda b, i: (b, 0, i))],
        out_specs=pl.BlockSpec((1, tq, NSA_Q_W), lambda b, i: (b, i, 0)),
        out_shape=jax.ShapeDtypeStruct((B, S, NSA_Q_W), BF16),
        scratch_shapes=[pltpu.VMEM((2 * D + LANES, NSA_HEADS * tq), BF16),
                        pltpu.VMEM((D + 2 * SUBLANES, NSA_HEADS * tq), F32),
                        pltpu.VMEM((1, NSA_HEADS * tq), F32),
                        pltpu.VMEM((D, NSA_HEADS * tq), F32)],
        compiler_params=pltpu.CompilerParams(dimension_semantics=("arbitrary", "arbitrary"),
                                             vmem_limit_bytes=VMEM_LIMIT),
        name="nsa_attn",
    )(qT, kcmp, vcT, jnp.asarray(ovl, BF16), jnp.asarray(eall, BF16), ks.reshape(B, S, kvw),
      vsT, kw.reshape(B, S, kvw), vwT, gateT)

    g3 = lambda a: a.reshape(B, S, a.shape[-1])
    gspec = lambda w: pl.BlockSpec((1, tg, w), lambda b, t: (b, t, 0))
    o_gla = pl.pallas_call(
        functools.partial(_gla_kernel, tg=tg),
        grid=(B, S // tg),
        in_specs=[gspec(GLA_QK_W), gspec(GLA_QK_W), gspec(GLA_QK_W), gspec(GLA_V_W), gspec(GLA_V_W),
                  _const_spec((1, GLA_DV))],
        out_specs=gspec(GLA_V_W),
        out_shape=jax.ShapeDtypeStruct((B, S, GLA_V_W), BF16),
        scratch_shapes=[pltpu.VMEM((GLA_HEADS, GLA_DV, GLA_QK_W), F32)],
        compiler_params=pltpu.CompilerParams(dimension_semantics=("arbitrary", "arbitrary"),
                                             vmem_limit_bytes=VMEM_LIMIT),
        name="gla",
    )(g3(gq), g3(gk), g3(la), g3(gv), g3(gout), out_gain.reshape(1, GLA_DV))

    n_t = (B * S) // tm
    row = lambda w: pl.BlockSpec((tm, w), lambda t: (t, 0))
    hmid = pl.pallas_call(
        _out_proj_kernel,
        grid=(n_t,),
        in_specs=[row(DM), row(NSA_Q_W), row(GLA_V_W), _const_spec((NSA_Q_W + GLA_V_W, DM))],
        out_specs=row(DM),
        out_shape=jax.ShapeDtypeStruct((B * S, DM), F32),
        compiler_params=pltpu.CompilerParams(dimension_semantics=("arbitrary",), vmem_limit_bytes=VMEM_LIMIT),
        name="out_proj",
    )(x2, o_nsa.reshape(B * S, NSA_Q_W), o_gla.reshape(B * S, GLA_V_W), w_out.astype(BF16))

    tf = 256
    assert D_FF % tf == 0
    halo_blocks = tm // SUBLANES
    out = pl.pallas_call(
        functools.partial(_ffn_kernel, tm=tm, tf=tf, tiles_per_seq=n_s),
        grid=(n_t,),
        in_specs=[row(DM),
                  pl.BlockSpec((SUBLANES, DM), lambda t: (jnp.maximum(t * halo_blocks - 1, 0), 0)),
                  _const_spec((1, DM)),
                  pl.BlockSpec((DM, 2 * D_FF), lambda t: (0, 0), pipeline_mode=pl.Buffered(1)),
                  _const_spec((CONV_W, 2 * D_FF)), _const_spec((1, 2 * D_FF)),
                  pl.BlockSpec((D_FF, DM), lambda t: (0, 0), pipeline_mode=pl.Buffered(1))],
        out_specs=row(DM),
        out_shape=jax.ShapeDtypeStruct((B * S, DM), F32),
        compiler_params=pltpu.CompilerParams(dimension_semantics=("arbitrary",), vmem_limit_bytes=VMEM_LIMIT),
        name="ffn",
    )(hmid, hmid, ffn_norm.reshape(1, DM), w_up.astype(BF16), conv_w, conv_b.reshape(1, 2 * D_FF),
      w_down.astype(BF16))
    return out.reshape(B, S, DM)
```

```python
import functools

import numpy as np
import jax
import jax.numpy as jnp
from jax import lax
from jax.experimental import pallas as pl
from jax.experimental.pallas import tpu as pltpu

F32 = jnp.float32
BF16 = jnp.bfloat16

NSA_HEADS = 8
NSA_KV_GROUPS = 2
NSA_HEAD_DIM = 64
NSA_HPG = NSA_HEADS // NSA_KV_GROUPS
CMP_BLOCK = 32
CMP_STRIDE = 16
CMP_HIDDEN = 256
SLC_BLOCK = 64
SLC_TOPK = 8
WINDOW = 512
FORCE_SCORE = 1e4
GLA_HEADS = 4
GLA_DK = 64
GLA_DV = 128
GLA_GATE_RANK = 16
GLA_TAU = 16.0
GLA_CHUNK = 64
ROPE_THETA = 500000.0
ROT_DIM = NSA_HEAD_DIM // 4
ROT_HALF = ROT_DIM // 2
D_FF = 2816
CONV_W = 3
EPS = 1e-6
NEG = -1e30
LOG2E = 1.4426950408889634

NSA_Q_W = NSA_HEADS * NSA_HEAD_DIM
NSA_KV_W = NSA_KV_GROUPS * NSA_HEAD_DIM
GLA_QK_W = GLA_HEADS * GLA_DK
GLA_V_W = GLA_HEADS * GLA_DV
IN_SPLITS = [NSA_Q_W, 6 * NSA_KV_W, 3 * NSA_HEADS, GLA_QK_W, GLA_QK_W, GLA_V_W, GLA_GATE_RANK, GLA_V_W]

LANES = 128
SUBLANES = 8
GATE_ROWS = 32
VMEM_LIMIT = 56 * 1024 * 1024


def _dot(a, b):
    return jnp.dot(a, b, preferred_element_type=F32)


def _dot_nt(a, b):
    return lax.dot_general(a, b, (((1,), (1,)), ((), ())), preferred_element_type=F32)


def _split_dot(m_bf16, x_f32):
    hi = x_f32.astype(BF16)
    lo = (x_f32 - hi.astype(F32)).astype(BF16)
    return _dot(m_bf16, hi) + _dot(m_bf16, lo)


def _split_dot_r(x_f32, m_bf16):
    hi = x_f32.astype(BF16)
    lo = (x_f32 - hi.astype(F32)).astype(BF16)
    return _dot(hi, m_bf16) + _dot(lo, m_bf16)


def _rope_lanes(y, c, a, b):
    return y * c + pltpu.roll(y, LANES - ROT_HALF, 1) * a + pltpu.roll(y, ROT_HALF, 1) * b


def _sigmoid(z):
    return 1.0 / (1.0 + jnp.exp(-z))


def _in_proj_kernel(x_ref, g_ref, wf_ref, wt_ref, qg_ref, kgs_ref, kgw_ref, bd_ref,
                    cosT_ref, sinT_ref, rc_ref, ra_ref, rb_ref, wg2_ref, bg_ref,
                    qT_ref, vcat_ref, gateT_ref, kcat_ref, kvc_ref,
                    gq_ref, gk_ref, la_ref, gv_ref, gout_ref):
    tm = x_ref.shape[0]
    x = x_ref[...]
    ms = jnp.mean(x * x, axis=-1, keepdims=True)
    xn = (x * lax.rsqrt(ms + EPS) * g_ref[...]).astype(BF16)

    zf = _dot_nt(wf_ref[...], xn)
    cosT = cosT_ref[...]
    sinT = sinT_ref[...]
    qg = qg_ref[...]
    scale = NSA_HEAD_DIM ** -0.5 * LOG2E
    for h in range(NSA_HEADS):
        blk = zf[h * NSA_HEAD_DIM:(h + 1) * NSA_HEAD_DIM]
        hms = jnp.mean(blk * blk, axis=0, keepdims=True)
        y = blk * lax.rsqrt(hms + EPS) * qg
        y1 = y[0:ROT_HALF]
        y2 = y[ROT_HALF:ROT_DIM]
        r = jnp.concatenate([y1 * cosT - y2 * sinT, y2 * cosT + y1 * sinT, y[ROT_DIM:]], axis=0)
        qT_ref[0, h * NSA_HEAD_DIM:(h + 1) * NSA_HEAD_DIM, :] = (r * scale).astype(BF16)
    o = NSA_Q_W
    vcat_ref[0, 0] = zf[o:o + NSA_KV_W].astype(BF16)
    vcat_ref[0, 1] = zf[o + NSA_KV_W:o + 2 * NSA_KV_W].astype(BF16)
    gateT_ref[0] = _sigmoid(zf[o + 2 * NSA_KV_W:o + 2 * NSA_KV_W + GATE_ROWS])

    bd = bd_ref[...]
    rc, ra, rb = rc_ref[...], ra_ref[...], rb_ref[...]

    def seg(a, b):
        return _dot(xn, wt_ref[:, a:b])

    def knorm(z, gain):
        gms = _split_dot_r(z * z, bd)
        return _rope_lanes(z * lax.rsqrt(gms + EPS) * gain, rc, ra, rb).astype(BF16)

    pos = pl.program_id(0) * tm + lax.broadcasted_iota(jnp.int32, (tm, LANES), 0)
    lane = lax.broadcasted_iota(jnp.int32, (tm, LANES), 1)
    onehot = jnp.where(pos // SLC_BLOCK == lane, 1.0, 0.0).astype(BF16)
    zl = jnp.zeros((tm, LANES), BF16)
    c = 0
    kcat_ref[0, 0] = jnp.concatenate([knorm(seg(c, c + NSA_KV_W), kgs_ref[...]), onehot], axis=1); c += NSA_KV_W
    kcat_ref[0, 1] = jnp.concatenate([knorm(seg(c, c + NSA_KV_W), kgw_ref[...]), zl], axis=1); c += NSA_KV_W
    kcat_ref[0, 2] = jnp.concatenate([zl, jnp.where(lane == LANES - 1, 1.0, 0.0).astype(BF16)], axis=1)
    kvc_ref[...] = seg(c, c + 2 * NSA_KV_W).astype(BF16); c += 2 * NSA_KV_W
    gq_ref[...] = seg(c, c + GLA_QK_W); c += GLA_QK_W
    gk_ref[...] = seg(c, c + GLA_QK_W); c += GLA_QK_W
    gv_ref[...] = seg(c, c + GLA_V_W).astype(BF16); c += GLA_V_W
    gout_ref[...] = seg(c, c + GLA_V_W).astype(BF16); c += GLA_V_W
    glr = seg(c, c + LANES).astype(BF16)
    pre = _dot(glr, wg2_ref[...]) + bg_ref[...]
    la_ref[...] = (jnp.minimum(pre, 0.0) - jnp.log1p(jnp.exp(-jnp.abs(pre)))) * (1.0 / GLA_TAU)


def _gelu_tanh(x):
    return 0.5 * x * (1.0 + jnp.tanh(0.7978845608028654 * (x + 0.044715 * (x * x * x))))


def _compress_kernel(r_ref, w1k_ref, w2k_ref, w1v_ref, w2v_ref, pek_ref, pev_ref, kg_ref,
                     cc_ref, ca_ref, cb_ref, kcmp_ref, vcT_ref, *, nc):
    half = CMP_STRIDE * NSA_HEAD_DIM
    for j in range(2 * NSA_KV_GROUPS):
        is_k = j < NSA_KV_GROUPS
        g = j % NSA_KV_GROUPS
        w1 = w1k_ref if is_k else w1v_ref
        w2 = w2k_ref if is_k else w2v_ref
        pe = pek_ref if is_k else pev_ref
        r = r_ref[0, j]
        p1 = _dot(r, w1[0:half, :])
        p2 = _dot(r, w1[half:2 * half, :])
        cst = _dot(pe[...], w1[...])[0:1]
        hid = p1 + pltpu.roll(p2, nc - 1, 0) + cst
        out = _dot(_gelu_tanh(hid).astype(BF16), w2[...])
        if is_k:
            kms = jnp.sum(out * out, axis=-1, keepdims=True) * (1.0 / NSA_HEAD_DIM)
            y = out * lax.rsqrt(kms + EPS) * kg_ref[...]
            kcmp_ref[0, g] = _rope_lanes(y, cc_ref[...], ca_ref[...], cb_ref[...]).astype(BF16)
        else:
            vcT_ref[0, g] = out.T[0:NSA_HEAD_DIM].astype(BF16)


def _nsa_kernel(qT_ref, kcmp_ref, vcT_ref, ovl_ref, kcat_ref, vcat_ref, gate_ref, o_ref,
                qall_ref, s0_ref, s1_ref, p0_ref, p1_ref, a0_ref, a1_ref, acc_ref, m_ref, out_ref,
                *, tq_size, nc, n_cmp, ns, top_k):
    TQ = tq_size
    TK = tq_size
    D = NSA_HEAD_DIM
    G = NSA_KV_GROUPS
    GW = NSA_HPG * TQ
    W = NSA_HEADS * TQ
    WIN, SEL = 0, 1
    P_SEL, P_WIN, P_NONE = 0, 1, 2
    n_back = WINDOW // TK
    i = pl.program_id(1)
    t0 = i * TQ
    s_bufs, p_bufs, a_bufs = (s0_ref, s1_ref), (p0_ref, p1_ref), (a0_ref, a1_ref)

    zeros = jnp.zeros((D, TQ), BF16)
    for gh in range(NSA_HEADS):
        g = gh // NSA_HPG
        cols = slice(gh * TQ, (gh + 1) * TQ)
        qall_ref[g * D:(g + 1) * D, cols] = qT_ref[0, gh * D:(gh + 1) * D, :]
        qall_ref[(1 - g) * D:(2 - g) * D, cols] = zeros
    brow = lax.broadcasted_iota(jnp.int32, (LANES, W), 0)
    qall_ref[2 * D:2 * D + LANES, :] = jnp.where(brow == LANES - 1, NEG, 0.0).astype(BF16)
    m_ref[...] = jnp.full(m_ref.shape, NEG, F32)
    acc_ref[...] = jnp.zeros(acc_ref.shape, F32)

    row = lax.broadcasted_iota(jnp.int32, (TK, W), 0)
    qcol = lax.broadcasted_iota(jnp.int32, (TK, W), 1) & (TQ - 1)
    ones_rows = jnp.ones((2 * SUBLANES, TK), BF16)

    def stage_qk(plane, k0, s_out):
        s_out[...] = _dot(kcat_ref[0, plane, pl.ds(pl.multiple_of(k0, TK), TK), :], qall_ref[...])

    def stage_sm(state, s_in, p_out, a_out, mask=None):
        s = s_in[...]
        if mask is not None:
            s = jnp.where(mask, s, NEG)
        m_old = m_ref[state]
        m_new = jnp.maximum(m_old, jnp.max(s, axis=0, keepdims=True))
        a_out[...] = jnp.exp2(m_old - m_new)
        m_ref[state] = m_new
        p_out[...] = jnp.exp2(s - m_new).astype(BF16)

    def stage_pv(state, plane, k0, p_in, a_in):
        vT = vcat_ref[0, jnp.minimum(plane, 1), :, pl.ds(pl.multiple_of(k0, TK), TK)]
        pv = jnp.concatenate(
            [_dot(jnp.concatenate([vT[g * D:(g + 1) * D], ones_rows], axis=0), p_in[:, g * GW:(g + 1) * GW])
             for g in range(G)], axis=1)
        acc_ref[state] = a_in[...] * acc_ref[state] + pv

    n_full = jnp.minimum(i, n_back - 1)
    n_unmasked = n_full + i
    n_jobs = 3 + n_unmasked + (1 - n_unmasked % 2)

    def job(j):
        u = j - 3
        is_full = u < n_full
        is_sel = jnp.logical_and(u >= n_full, u - n_full < i)
        plane = jnp.where(is_full, P_WIN, jnp.where(is_sel, P_SEL, P_NONE))
        kv = jnp.where(is_full, i - 1 - u, jnp.where(is_sel, u - n_full, 0))
        state = jnp.where(is_full, WIN, SEL)
        plane = jnp.where(j == 0, P_WIN, jnp.where(j == 1, P_SEL, jnp.where(j == 2, P_WIN, plane)))
        kv = jnp.where(j < 2, i, jnp.where(j == 2, jnp.maximum(i - n_back, 0), kv))
        state = jnp.where(j == 0, WIN, jnp.where(j == 1, SEL, jnp.where(j == 2, WIN, state)))
        return state, plane, kv * TK

    def run(it_pv, it_sm, it_qk, par, mask=None):
        if it_pv is not None:
            st, plane, k0 = job(it_pv)
            stage_pv(st, plane, k0, p_bufs[par], a_bufs[par])
        if it_sm is not None:
            st, _, _ = job(it_sm)
            stage_sm(st, s_bufs[1 - par], p_bufs[1 - par], a_bufs[1 - par], mask)
        if it_qk is not None:
            _, plane, k0 = job(it_qk)
            stage_qk(plane, k0, s_bufs[par])

    run(None, None, 0, 0)

    def gate_vec(c):
        return jnp.concatenate([gate_ref[0, gh * 3 + c:gh * 3 + c + 1, :] for gh in range(NSA_HEADS)], axis=1)

    tq = t0 + lax.broadcasted_iota(jnp.int32, (1, TQ), 1)
    tq_g = t0 + (lax.broadcasted_iota(jnp.int32, (1, GW), 1) & (TQ - 1))
    n_idx = lax.broadcasted_iota(jnp.int32, (nc, GW), 0)
    cmask = (n_idx * CMP_STRIDE + (CMP_BLOCK - 1) <= tq_g) & (n_idx < n_cmp)
    j_idx = lax.broadcasted_iota(jnp.int32, (ns, TQ), 0)
    cur = tq // SLC_BLOCK
    forced = (j_idx == 0) | (j_idx == cur) | (j_idx == cur - 1)
    zpad = jnp.zeros((D, GW), BF16)
    for g in range(G):
        gcols = slice(g * GW, (g + 1) * GW)
        q128 = jnp.concatenate([qall_ref[g * D:(g + 1) * D, gcols], zpad], axis=0)
        s = jnp.where(cmask, _dot(kcmp_ref[0, g], q128), NEG)
        m = jnp.max(s, axis=0, keepdims=True)
        e = jnp.where(cmask, jnp.exp2(s - m), 0.0)
        l = jnp.sum(e, axis=0, keepdims=True)
        p = e * jnp.where(l > 0.0, 1.0 / l, 0.0)
        out_ref[:, gcols] = _dot(vcT_ref[0, g], p.astype(BF16))
        psum = p[:, 0:TQ]
        for h in range(1, NSA_HPG):
            psum = psum + p[:, h * TQ:(h + 1) * TQ]
        p_slc = _split_dot(ovl_ref[...], psum)
        score = jnp.where(j_idx <= cur, jnp.where(forced, FORCE_SCORE, p_slc), -jnp.inf)
        rank = jnp.zeros((ns, TQ), F32)
        for a in range(ns):
            sa = score[a:a + 1, :]
            beats = (sa > score) | ((sa == score) & (j_idx > a))
            rank = rank + beats.astype(F32)
        selbias = jnp.where(rank < top_k, 0.0, NEG).astype(BF16)
        qall_ref[2 * D:2 * D + ns, gcols] = jnp.concatenate([selbias] * NSA_HPG, axis=1)
    out_ref[...] = out_ref[...] * gate_vec(0)

    diag = row <= qcol
    tail = (row > qcol) & (i >= n_back)
    run(None, 0, 1, 1, diag)
    run(0, 1, 2, 0, diag)
    run(1, 2, 3, 1, tail)

    def pair(v, carry):
        a = 3 + 2 * v
        run(a - 1, a, a + 1, 0)
        run(a, a + 1, a + 2, 1)
        return carry

    lax.fori_loop(0, (n_jobs - 4) // 2, pair, 0)
    run(n_jobs - 2, n_jobs - 1, None, 0)
    run(n_jobs - 1, None, None, 1)

    def branch_out(state, c):
        return gate_vec(c) * (acc_ref[state, 0:D, :] * (1.0 / acc_ref[state, D:D + 1, :]))

    out = out_ref[...] + branch_out(SEL, 1) + branch_out(WIN, 2)
    stacked = jnp.concatenate([out[:, gh * TQ:(gh + 1) * TQ] for gh in range(NSA_HEADS)], axis=0)
    o_ref[0] = stacked.T.astype(BF16)


def _gla_kernel(gq_ref, gk_ref, la_ref, gv_ref, gout_ref, gain_ref, o_ref, st_ref, *, tg):
    L = GLA_CHUNK

    @pl.when(pl.program_id(1) == 0)
    def _():
        st_ref[...] = jnp.zeros_like(st_ref)

    la = la_ref[0]
    row = lax.broadcasted_iota(jnp.int32, (tg, tg), 0)
    col = lax.broadcasted_iota(jnp.int32, (tg, tg), 1)
    same = (row // L) == (col // L)
    tril = same & (col <= row)
    b = _split_dot(tril.astype(BF16), la)
    bl = _split_dot(same.astype(BF16), la)
    gk = gk_ref[0]
    qt = (gq_ref[0] * (GLA_DK ** -0.5) * jnp.exp(b))
    kt = (gk * jnp.exp(-b)).astype(BF16)
    kd = (gk * jnp.exp(bl - b)).astype(BF16)
    dec = jnp.exp(bl)
    lane = lax.broadcasted_iota(jnp.int32, (1, GLA_QK_W), 1)
    gain = gain_ref[...]
    for h in range(GLA_HEADS):
        qth = jnp.where((lane // GLA_DK) == h, qt, 0.0).astype(BF16)
        a = jnp.where(tril, _dot_nt(qth, kt), 0.0)
        vh = gv_ref[0, :, h * GLA_DV:(h + 1) * GLA_DV]
        o_intra = _dot(a.astype(BF16), vh)
        pieces = []
        for c in range(tg // L):
            rows = slice(c * L, (c + 1) * L)
            st = st_ref[h]
            pieces.append(o_intra[rows] + _dot_nt(qth[rows], st.astype(BF16)))
            vT = vh[rows].astype(F32).T.astype(BF16)
            st_ref[h] = st * dec[c * L:c * L + 1] + _dot(vT, kd[rows])
        o = jnp.concatenate(pieces, axis=0)
        oms = jnp.mean(o * o, axis=-1, keepdims=True)
        y = o * lax.rsqrt(oms + EPS) * gain
        go = gout_ref[0, :, h * GLA_DV:(h + 1) * GLA_DV].astype(F32)
        o_ref[0, :, h * GLA_DV:(h + 1) * GLA_DV] = (y * (go * _sigmoid(go))).astype(BF16)


def _out_proj_kernel(x_ref, on_ref, og_ref, wo_ref, h_ref):
    h_ref[...] = (x_ref[...] + _dot(on_ref[...], wo_ref[0:NSA_Q_W, :])
                  + _dot(og_ref[...], wo_ref[NSA_Q_W:NSA_Q_W + GLA_V_W, :]))


def _ffn_kernel(h_ref, halo_ref, g_ref, wup_ref, cw_ref, cb_ref, wdn_ref, o_ref, u_ref, act_ref,
                *, tm, tf, tiles_per_seq):
    h = h_ref[...]
    hh = jnp.concatenate([halo_ref[...], h], axis=0)
    ms = jnp.mean(hh * hh, axis=-1, keepdims=True)
    hn = (hh * lax.rsqrt(ms + EPS) * g_ref[...]).astype(BF16)
    halo_keep = jnp.where((pl.program_id(0) % tiles_per_seq) == 0, 0.0, 1.0)
    lo, hi = SUBLANES, tm + SUBLANES

    def conv(col0, slot):
        u = _dot(hn, wup_ref[:, col0:col0 + tf])
        u_ref[slot, lo:hi, :] = u[lo:hi]
        u_ref[slot, 0:lo, :] = u[0:lo] * halo_keep
        w = cw_ref[:, col0:col0 + tf]
        return (u[lo:hi] * w[2:3] + u_ref[slot, lo - 1:hi - 1, :] * w[1:2]
                + u_ref[slot, lo - 2:hi - 2, :] * w[0:1] + cb_ref[:, col0:col0 + tf])

    for c in range(D_FF // tf):
        gate = conv(c * tf, 2 * (c % 2))
        up = conv(D_FF + c * tf, 2 * (c % 2) + 1)
        act_ref[:, c * tf:(c + 1) * tf] = (gate * _sigmoid(gate) * up).astype(BF16)
    o_ref[...] = h + _dot(act_ref[...], wdn_ref[...])


def _rope_tables(pos):
    inv = ROPE_THETA ** (-np.arange(ROT_HALF, dtype=np.float32) / ROT_HALF)
    ang = pos.astype(np.float32)[:, None] * inv[None, :]
    cos, sin = np.cos(ang), np.sin(ang)
    n = len(pos)
    c = np.ones((n, LANES), np.float32)
    a = np.zeros((n, LANES), np.float32)
    b = np.zeros((n, LANES), np.float32)
    for base in range(0, LANES, NSA_HEAD_DIM):
        c[:, base:base + ROT_HALF] = cos
        c[:, base + ROT_HALF:base + ROT_DIM] = cos
        a[:, base:base + ROT_HALF] = -sin
        b[:, base + ROT_HALF:base + ROT_DIM] = sin
    return cos, sin, c, a, b


def _const_spec(shape):
    nd = len(shape)
    return pl.BlockSpec(shape, lambda *_: (0,) * nd)


def kernel(x, attn_norm, w_in, nsa_q_norm, nsa_k_norm, cmp_pos_k, cmp_pos_v, cmp_w1_k, cmp_w2_k,
           cmp_w1_v, cmp_w2_v, gla_w_gate2, gla_b_gate, gla_out_norm, w_out, ffn_norm, w_up,
           conv_w, conv_b, w_down):
    B, S, DM = x.shape
    depth = attn_norm.shape[0]
    h = x
    for l in range(depth):
        h = _layer(h, attn_norm[l], w_in[l], nsa_q_norm[l], nsa_k_norm[l], cmp_pos_k[l], cmp_pos_v[l],
                   cmp_w1_k[l], cmp_w2_k[l], cmp_w1_v[l], cmp_w2_v[l], gla_w_gate2[l], gla_b_gate[l],
                   gla_out_norm[l], w_out[l], ffn_norm[l], w_up[l], conv_w[l], conv_b[l], w_down[l])
    return h


def _layer(x, attn_norm, w_in, q_gain, k_gains, pe_k, pe_v, w1_k, w2_k, w1_v, w2_v, w_gate2, b_gate,
           out_gain, w_out, ffn_norm, w_up, conv_w, conv_b, w_down):
    B, S, DM = x.shape
    G, D = NSA_KV_GROUPS, NSA_HEAD_DIM
    tm = min(512, S)
    tq = min(256, S)
    tg = min(256, S)
    n_s = S // tm
    nc = S // CMP_STRIDE
    n_cmp = (S - CMP_BLOCK) // CMP_STRIDE + 1
    ns = S // SLC_BLOCK
    top_k = min(SLC_TOPK, ns)
    assert S % tm == 0 and S % tq == 0 and WINDOW % tq == 0 and ns <= LANES and nc % SUBLANES == 0

    offs = np.cumsum([0] + IN_SPLITS)
    nq0, nkv0, ngate0, gq0, gk0, gv0, glr0, gout0 = [int(o) for o in offs[:-1]]
    kvw = NSA_KV_W
    cols = lambda a, n: w_in[:, a:a + n]
    w_feat = jnp.concatenate([cols(nq0, NSA_Q_W), cols(nkv0 + 3 * kvw, kvw), cols(nkv0 + 5 * kvw, kvw),
                              cols(ngate0, 3 * NSA_HEADS),
                              jnp.zeros((DM, GATE_ROWS - 3 * NSA_HEADS), w_in.dtype)], axis=1)
    wf = w_feat.T.astype(BF16)
    w_tok = jnp.concatenate([cols(nkv0 + 2 * kvw, kvw), cols(nkv0 + 4 * kvw, kvw),
                             cols(nkv0, kvw), cols(nkv0 + kvw, kvw),
                             cols(gq0, GLA_QK_W), cols(gk0, GLA_QK_W), cols(gv0, GLA_V_W),
                             cols(gout0, GLA_V_W), cols(glr0, GLA_GATE_RANK),
                             jnp.zeros((DM, LANES - GLA_GATE_RANK), w_in.dtype)], axis=1)
    wt = w_tok.astype(BF16)
    nf, nt = wf.shape[0], wt.shape[1]
    wg2 = jnp.concatenate([w_gate2, jnp.zeros((LANES - GLA_GATE_RANK, GLA_QK_W), w_gate2.dtype)],
                          axis=0).astype(BF16)
    cos, sin, rc, ra, rb = _rope_tables(np.arange(S))
    cosT, sinT = jnp.asarray(cos.T), jnp.asarray(sin.T)
    cmp_end = np.arange(nc) * CMP_STRIDE + CMP_BLOCK - 1
    _, _, cc, ca, cb = _rope_tables(cmp_end)
    bd = np.kron(np.eye(LANES // D, dtype=np.float32), np.full((D, D), 1.0 / D, np.float32))
    tile2 = lambda v: jnp.tile(v.reshape(1, D), (1, LANES // D))

    x2 = x.reshape(B * S, DM)
    tok = lambda w: pl.BlockSpec((tm, w), lambda s, b: (b * n_s + s, 0))
    feat = lambda r: pl.BlockSpec((1, r, tm), lambda s, b: (b, 0, s))
    stab = lambda r, w: pl.BlockSpec((r, w), lambda s, b: (0, s)) if r == ROT_HALF else \
        pl.BlockSpec((r, w), lambda s, b: (s, 0))
    outs = pl.pallas_call(
        _in_proj_kernel,
        grid=(n_s, B),
        in_specs=[tok(DM), _const_spec((1, DM)), _const_spec((nf, DM)), _const_spec((DM, nt)),
                  _const_spec((D, 1)), _const_spec((1, LANES)), _const_spec((1, LANES)),
                  _const_spec((LANES, LANES)),
                  stab(ROT_HALF, tm), stab(ROT_HALF, tm), stab(tm, LANES), stab(tm, LANES), stab(tm, LANES),
                  _const_spec((LANES, GLA_QK_W)), _const_spec((1, GLA_QK_W))],
        out_specs=[feat(NSA_Q_W), pl.BlockSpec((1, 2, kvw, tm), lambda s, b: (b, 0, 0, s)), feat(GATE_ROWS),
                   pl.BlockSpec((1, 3, tm, 2 * LANES), lambda s, b: (b, 0, s, 0)),
                   tok(2 * kvw), tok(GLA_QK_W), tok(GLA_QK_W), tok(GLA_QK_W),
                   tok(GLA_V_W), tok(GLA_V_W)],
        out_shape=[jax.ShapeDtypeStruct((B, NSA_Q_W, S), BF16), jax.ShapeDtypeStruct((B, 2, kvw, S), BF16),
                   jax.ShapeDtypeStruct((B, GATE_ROWS, S), F32),
                   jax.ShapeDtypeStruct((B, 3, S, 2 * LANES), BF16),
                   jax.ShapeDtypeStruct((B * S, 2 * kvw), BF16),
                   jax.ShapeDtypeStruct((B * S, GLA_QK_W), F32), jax.ShapeDtypeStruct((B * S, GLA_QK_W), F32),
                   jax.ShapeDtypeStruct((B * S, GLA_QK_W), F32),
                   jax.ShapeDtypeStruct((B * S, GLA_V_W), BF16), jax.ShapeDtypeStruct((B * S, GLA_V_W), BF16)],
        compiler_params=pltpu.CompilerParams(dimension_semantics=("arbitrary", "arbitrary"),
                                             vmem_limit_bytes=VMEM_LIMIT),
        name="in_proj",
    )(x2, attn_norm.reshape(1, DM), wf, wt, q_gain.reshape(D, 1), tile2(k_gains[1]), tile2(k_gains[2]),
      jnp.asarray(bd, BF16), cosT, sinT, jnp.asarray(rc), jnp.asarray(ra), jnp.asarray(rb),
      wg2, b_gate.reshape(1, GLA_QK_W))
    qT, vcat, gateT, kcat, kvc, gq, gk, la, gv, gout = outs

    r = kvc.reshape(B, nc, CMP_STRIDE, 2 * G, D).transpose(0, 3, 1, 2, 4).reshape(B, 2 * G, nc, CMP_STRIDE * D)
    pad_w2 = lambda w: jnp.concatenate([w, jnp.zeros((CMP_HIDDEN, LANES - D), w.dtype)], axis=1).astype(BF16)
    pe_rows = lambda p: jnp.tile(p.reshape(1, CMP_BLOCK * D), (SUBLANES, 1)).astype(BF16)
    cin = CMP_BLOCK * D
    kcmp, vcT = pl.pallas_call(
        functools.partial(_compress_kernel, nc=nc),
        grid=(B,),
        in_specs=[pl.BlockSpec((1, 2 * G, nc, CMP_STRIDE * D), lambda b: (b, 0, 0, 0)),
                  _const_spec((cin, CMP_HIDDEN)), _const_spec((CMP_HIDDEN, LANES)),
                  _const_spec((cin, CMP_HIDDEN)), _const_spec((CMP_HIDDEN, LANES)),
                  _const_spec((SUBLANES, cin)), _const_spec((SUBLANES, cin)), _const_spec((1, LANES)),
                  _const_spec((nc, LANES)), _const_spec((nc, LANES)), _const_spec((nc, LANES))],
        out_specs=[pl.BlockSpec((1, G, nc, LANES), lambda b: (b, 0, 0, 0)),
                   pl.BlockSpec((1, G, D, nc), lambda b: (b, 0, 0, 0))],
        out_shape=[jax.ShapeDtypeStruct((B, G, nc, LANES), BF16), jax.ShapeDtypeStruct((B, G, D, nc), BF16)],
        compiler_params=pltpu.CompilerParams(dimension_semantics=("arbitrary",), vmem_limit_bytes=VMEM_LIMIT),
        name="compress",
    )(r, w1_k.astype(BF16), pad_w2(w2_k), w1_v.astype(BF16), pad_w2(w2_v), pe_rows(pe_k), pe_rows(pe_v),
      tile2(k_gains[0]), jnp.asarray(cc), jnp.asarray(ca), jnp.asarray(cb))

    ci = np.arange(nc)[None, :]
    sj = np.arange(ns)[:, None]
    ovl = ((ci * CMP_STRIDE < (sj + 1) * SLC_BLOCK) & (ci * CMP_STRIDE + CMP_BLOCK > sj * SLC_BLOCK)
           & (ci < n_cmp)).astype(np.float32)
    n_q = S // tq
    wq = NSA_HEADS * tq
    o_nsa = pl.pallas_call(
        functools.partial(_nsa_kernel, tq_size=tq, nc=nc, n_cmp=n_cmp, ns=ns, top_k=top_k),
        grid=(B, n_q),
        in_specs=[pl.BlockSpec((1, NSA_Q_W, tq), lambda b, i: (b, 0, i)),
                  pl.BlockSpec((1, G, nc, LANES), lambda b, i: (b, 0, 0, 0)),
                  pl.BlockSpec((1, G, D, nc), lambda b, i: (b, 0, 0, 0)),
                  _const_spec((ns, nc)),
                  pl.BlockSpec((1, 3, S, 2 * LANES), lambda b, i: (b, 0, 0, 0)),
                  pl.BlockSpec((1, 2, kvw, S), lambda b, i: (b, 0, 0, 0)),
                  pl.BlockSpec((1, GATE_ROWS, tq), lambda b, i: (b, 0, i))],
        out_specs=pl.BlockSpec((1, tq, NSA_Q_W), lambda b, i: (b, i, 0)),
        out_shape=jax.ShapeDtypeStruct((B, S, NSA_Q_W), BF16),
        scratch_shapes=[pltpu.VMEM((2 * D + LANES, wq), BF16),
                        pltpu.VMEM((tq, wq), F32), pltpu.VMEM((tq, wq), F32),
                        pltpu.VMEM((tq, wq), BF16), pltpu.VMEM((tq, wq), BF16),
                        pltpu.VMEM((1, wq), F32), pltpu.VMEM((1, wq), F32),
                        pltpu.VMEM((2, D + 2 * SUBLANES, wq), F32),
                        pltpu.VMEM((2, 1, wq), F32),
                        pltpu.VMEM((D, wq), F32)],
        compiler_params=pltpu.CompilerParams(dimension_semantics=("arbitrary", "arbitrary"),
                                             vmem_limit_bytes=VMEM_LIMIT),
        name="nsa_attn",
    )(qT, kcmp, vcT, jnp.asarray(ovl, BF16), kcat, vcat, gateT)

    g3 = lambda a: a.reshape(B, S, a.shape[-1])
    gspec = lambda w: pl.BlockSpec((1, tg, w), lambda b, t: (b, t, 0))
    o_gla = pl.pallas_call(
        functools.partial(_gla_kernel, tg=tg),
        grid=(B, S // tg),
        in_specs=[gspec(GLA_QK_W), gspec(GLA_QK_W), gspec(GLA_QK_W), gspec(GLA_V_W), gspec(GLA_V_W),
                  _const_spec((1, GLA_DV))],
        out_specs=gspec(GLA_V_W),
        out_shape=jax.ShapeDtypeStruct((B, S, GLA_V_W), BF16),
        scratch_shapes=[pltpu.VMEM((GLA_HEADS, GLA_DV, GLA_QK_W), F32)],
        compiler_params=pltpu.CompilerParams(dimension_semantics=("arbitrary", "arbitrary"),
                                             vmem_limit_bytes=VMEM_LIMIT),
        name="gla",
    )(g3(gq), g3(gk), g3(la), g3(gv), g3(gout), out_gain.reshape(1, GLA_DV))

    n_t = (B * S) // tm
    row = lambda w: pl.BlockSpec((tm, w), lambda t: (t, 0))
    hmid = pl.pallas_call(
        _out_proj_kernel,
        grid=(n_t,),
        in_specs=[row(DM), row(NSA_Q_W), row(GLA_V_W), _const_spec((NSA_Q_W + GLA_V_W, DM))],
        out_specs=row(DM),
        out_shape=jax.ShapeDtypeStruct((B * S, DM), F32),
        compiler_params=pltpu.CompilerParams(dimension_semantics=("arbitrary",), vmem_limit_bytes=VMEM_LIMIT),
        name="out_proj",
    )(x2, o_nsa.reshape(B * S, NSA_Q_W), o_gla.reshape(B * S, GLA_V_W), w_out.astype(BF16))

    tf = 256
    assert D_FF % tf == 0
    halo_blocks = tm // SUBLANES
    out = pl.pallas_call(
        functools.partial(_ffn_kernel, tm=tm, tf=tf, tiles_per_seq=n_s),
        grid=(n_t,),
        in_specs=[row(DM),
                  pl.BlockSpec((SUBLANES, DM), lambda t: (jnp.maximum(t * halo_blocks - 1, 0), 0)),
                  _const_spec((1, DM)),
                  pl.BlockSpec((DM, 2 * D_FF), lambda t: (0, 0), pipeline_mode=pl.Buffered(1)),
                  _const_spec((CONV_W, 2 * D_FF)), _const_spec((1, 2 * D_FF)),
                  pl.BlockSpec((D_FF, DM), lambda t: (0, 0), pipeline_mode=pl.Buffered(1))],
        out_specs=row(DM),
        out_shape=jax.ShapeDtypeStruct((B * S, DM), F32),
        scratch_shapes=[pltpu.VMEM((4, tm + SUBLANES, tf), F32), pltpu.VMEM((tm, D_FF), BF16)],
        compiler_params=pltpu.CompilerParams(dimension_semantics=("arbitrary",), vmem_limit_bytes=VMEM_LIMIT),
        name="ffn",
    )(hmid, hmid, ffn_norm.reshape(1, DM), w_up.astype(BF16), conv_w, conv_b.reshape(1, 2 * D_FF),
      w_down.astype(BF16))
    return out.reshape(B, S, DM)
```

```python
import functools

import numpy as np
import jax
import jax.numpy as jnp
from jax import lax
from jax.experimental import pallas as pl
from jax.experimental.pallas import tpu as pltpu

F32 = jnp.float32
BF16 = jnp.bfloat16

NSA_HEADS = 8
NSA_KV_GROUPS = 2
NSA_HEAD_DIM = 64
NSA_HPG = NSA_HEADS // NSA_KV_GROUPS
CMP_BLOCK = 32
CMP_STRIDE = 16
CMP_HIDDEN = 256
SLC_BLOCK = 64
SLC_TOPK = 8
WINDOW = 512
FORCE_SCORE = 1e4
GLA_HEADS = 4
GLA_DK = 64
GLA_DV = 128
GLA_GATE_RANK = 16
GLA_TAU = 16.0
GLA_CHUNK = 64
ROPE_THETA = 500000.0
ROT_DIM = NSA_HEAD_DIM // 4
ROT_HALF = ROT_DIM // 2
D_FF = 2816
CONV_W = 3
EPS = 1e-6
NEG = -1e30
LOG2E = 1.4426950408889634

NSA_Q_W = NSA_HEADS * NSA_HEAD_DIM
NSA_KV_W = NSA_KV_GROUPS * NSA_HEAD_DIM
GLA_QK_W = GLA_HEADS * GLA_DK
GLA_V_W = GLA_HEADS * GLA_DV
IN_SPLITS = [NSA_Q_W, 6 * NSA_KV_W, 3 * NSA_HEADS, GLA_QK_W, GLA_QK_W, GLA_V_W, GLA_GATE_RANK, GLA_V_W]

LANES = 128
SUBLANES = 8
GATE_ROWS = 32
VMEM_LIMIT = 56 * 1024 * 1024


def _dot(a, b):
    return jnp.dot(a, b, preferred_element_type=F32)


def _dot_nt(a, b):
    return lax.dot_general(a, b, (((1,), (1,)), ((), ())), preferred_element_type=F32)


def _split_dot(m_bf16, x_f32):
    hi = x_f32.astype(BF16)
    lo = (x_f32 - hi.astype(F32)).astype(BF16)
    return _dot(m_bf16, hi) + _dot(m_bf16, lo)


def _split_dot_r(x_f32, m_bf16):
    hi = x_f32.astype(BF16)
    lo = (x_f32 - hi.astype(F32)).astype(BF16)
    return _dot(hi, m_bf16) + _dot(lo, m_bf16)


def _rope_lanes(y, c, a, b):
    return y * c + pltpu.roll(y, LANES - ROT_HALF, 1) * a + pltpu.roll(y, ROT_HALF, 1) * b


def _sigmoid(z):
    return 1.0 / (1.0 + jnp.exp(-z))


def _in_proj_kernel(x_ref, g_ref, wf_ref, wt_ref, qg_ref, kgs_ref, kgw_ref, bd_ref,
                    cosT_ref, sinT_ref, rc_ref, ra_ref, rb_ref, wg2_ref, bg_ref,
                    qT_ref, vcat_ref, gateT_ref, kcat_ref, kvc_ref,
                    gq_ref, gk_ref, la_ref, gv_ref, gout_ref):
    tm = x_ref.shape[0]
    x = x_ref[...]
    ms = jnp.mean(x * x, axis=-1, keepdims=True)
    xn = (x * lax.rsqrt(ms + EPS) * g_ref[...]).astype(BF16)

    offs = np.cumsum([0, NSA_KV_W, NSA_KV_W, 2 * NSA_KV_W, GLA_QK_W, GLA_QK_W, GLA_V_W, GLA_V_W, LANES])
    c_ks, c_kw, c_kvc, c_gq, c_gk, c_gv, c_gout, c_glr, c_end = [int(v) for v in offs]

    def seg(a, b):
        return _dot(xn, wt_ref[:, a:b])

    zf = _dot_nt(wf_ref[...], xn)
    z_ks = seg(c_ks, c_kw)
    z_kw = seg(c_kw, c_kvc)
    cosT = cosT_ref[...]
    sinT = sinT_ref[...]
    qg = qg_ref[...]
    scale = NSA_HEAD_DIM ** -0.5 * LOG2E

    def q_head(h):
        blk = zf[h * NSA_HEAD_DIM:(h + 1) * NSA_HEAD_DIM]
        hms = jnp.mean(blk * blk, axis=0, keepdims=True)
        y = blk * lax.rsqrt(hms + EPS) * qg
        y1 = y[0:ROT_HALF]
        y2 = y[ROT_HALF:ROT_DIM]
        r = jnp.concatenate([y1 * cosT - y2 * sinT, y2 * cosT + y1 * sinT, y[ROT_DIM:]], axis=0)
        qT_ref[0, h * NSA_HEAD_DIM:(h + 1) * NSA_HEAD_DIM, :] = (r * scale).astype(BF16)

    gq_ref[...] = seg(c_gq, c_gk)
    for h in range(0, NSA_HEADS // 2):
        q_head(h)
    gk_ref[...] = seg(c_gk, c_gv)
    for h in range(NSA_HEADS // 2, NSA_HEADS):
        q_head(h)
    gv_ref[...] = seg(c_gv, c_gout).astype(BF16)
    o = NSA_Q_W
    vcat_ref[0, 0] = zf[o:o + NSA_KV_W].astype(BF16)
    vcat_ref[0, 1] = zf[o + NSA_KV_W:o + 2 * NSA_KV_W].astype(BF16)
    gateT_ref[0] = _sigmoid(zf[o + 2 * NSA_KV_W:o + 2 * NSA_KV_W + GATE_ROWS])

    bd = bd_ref[...]
    rc, ra, rb = rc_ref[...], ra_ref[...], rb_ref[...]

    def knorm(z, gain):
        gms = _split_dot_r(z * z, bd)
        return _rope_lanes(z * lax.rsqrt(gms + EPS) * gain, rc, ra, rb).astype(BF16)

    pos = pl.program_id(0) * tm + lax.broadcasted_iota(jnp.int32, (tm, LANES), 0)
    lane = lax.broadcasted_iota(jnp.int32, (tm, LANES), 1)
    onehot = jnp.where(pos // SLC_BLOCK == lane, 1.0, 0.0).astype(BF16)
    zl = jnp.zeros((tm, LANES), BF16)
    gout_ref[...] = seg(c_gout, c_glr).astype(BF16)
    kcat_ref[0, 0] = jnp.concatenate([knorm(z_ks, kgs_ref[...]), onehot], axis=1)
    kvc_ref[...] = seg(c_kvc, c_gq).astype(BF16)
    kcat_ref[0, 1] = jnp.concatenate([knorm(z_kw, kgw_ref[...]), zl], axis=1)
    kcat_ref[0, 2] = jnp.concatenate([zl, jnp.where(lane == LANES - 1, 1.0, 0.0).astype(BF16)], axis=1)
    glr = seg(c_glr, c_end).astype(BF16)
    pre = _dot(glr, wg2_ref[...]) + bg_ref[...]
    la_ref[...] = (jnp.minimum(pre, 0.0) - jnp.log1p(jnp.exp(-jnp.abs(pre)))) * (1.0 / GLA_TAU)


def _gelu_tanh(x):
    return 0.5 * x * (1.0 + jnp.tanh(0.7978845608028654 * (x + 0.044715 * (x * x * x))))


def _compress_kernel(r_ref, w1k_ref, w2k_ref, w1v_ref, w2v_ref, pek_ref, pev_ref, kg_ref,
                     cc_ref, ca_ref, cb_ref, kcmp_ref, vcT_ref, *, nc):
    half = CMP_STRIDE * NSA_HEAD_DIM
    for j in range(2 * NSA_KV_GROUPS):
        is_k = j < NSA_KV_GROUPS
        g = j % NSA_KV_GROUPS
        w1 = w1k_ref if is_k else w1v_ref
        w2 = w2k_ref if is_k else w2v_ref
        pe = pek_ref if is_k else pev_ref
        r = r_ref[0, j]
        p1 = _dot(r, w1[0:half, :])
        p2 = _dot(r, w1[half:2 * half, :])
        cst = _dot(pe[...], w1[...])[0:1]
        hid = p1 + pltpu.roll(p2, nc - 1, 0) + cst
        out = _dot(_gelu_tanh(hid).astype(BF16), w2[...])
        if is_k:
            kms = jnp.sum(out * out, axis=-1, keepdims=True) * (1.0 / NSA_HEAD_DIM)
            y = out * lax.rsqrt(kms + EPS) * kg_ref[...]
            kcmp_ref[0, g] = _rope_lanes(y, cc_ref[...], ca_ref[...], cb_ref[...]).astype(BF16)
        else:
            vcT_ref[0, g] = out.T[0:NSA_HEAD_DIM].astype(BF16)


def _nsa_kernel(qT_ref, kcmp_ref, vcT_ref, ovl_ref, kcat_ref, vcat_ref, gate_ref, o_ref,
                qall_ref, s0_ref, s1_ref, p0_ref, p1_ref, a0_ref, a1_ref, acc_ref, m_ref, out_ref,
                *, tq_size, nc, n_cmp, ns, top_k):
    TQ = tq_size
    TK = tq_size
    D = NSA_HEAD_DIM
    G = NSA_KV_GROUPS
    GW = NSA_HPG * TQ
    W = NSA_HEADS * TQ
    WIN, SEL = 0, 1
    P_SEL, P_WIN, P_NONE = 0, 1, 2
    n_back = WINDOW // TK
    i = pl.program_id(1)
    t0 = i * TQ
    s_bufs, p_bufs, a_bufs = (s0_ref, s1_ref), (p0_ref, p1_ref), (a0_ref, a1_ref)

    zeros = jnp.zeros((D, TQ), BF16)
    for gh in range(NSA_HEADS):
        g = gh // NSA_HPG
        cols = slice(gh * TQ, (gh + 1) * TQ)
        qall_ref[g * D:(g + 1) * D, cols] = qT_ref[0, gh * D:(gh + 1) * D, :]
        qall_ref[(1 - g) * D:(2 - g) * D, cols] = zeros
    brow = lax.broadcasted_iota(jnp.int32, (LANES, W), 0)
    qall_ref[2 * D:2 * D + LANES, :] = jnp.where(brow == LANES - 1, NEG, 0.0).astype(BF16)
    m_ref[...] = jnp.full(m_ref.shape, NEG, F32)
    acc_ref[...] = jnp.zeros(acc_ref.shape, F32)

    row = lax.broadcasted_iota(jnp.int32, (TK, TQ), 0)
    qcol = lax.broadcasted_iota(jnp.int32, (TK, TQ), 1)
    ones_rows = jnp.ones((2 * SUBLANES, TK), BF16)

    def stage_qk(kt, cb, s_out):
        s_out[:, cb] = _dot(kt, qall_ref[:, cb])

    def stage_sm(state, cb, s_in, p_out, a_out, mask=None):
        s = s_in[:, cb]
        if mask is not None:
            s = jnp.where(mask, s, NEG)
        m_old = m_ref[state, :, cb]
        m_new = jnp.maximum(m_old, jnp.max(s, axis=0, keepdims=True))
        a_out[:, cb] = jnp.exp2(m_old - m_new)
        m_ref[state, :, cb] = m_new
        p_out[:, cb] = jnp.exp2(s - m_new).astype(BF16)

    def stage_pv(state, vaug, cb, p_in, a_in):
        acc_ref[state, :, cb] = a_in[:, cb] * acc_ref[state, :, cb] + _dot(vaug, p_in[:, cb])

    n_full = jnp.minimum(i, n_back - 1)
    n_unmasked = n_full + i
    n_jobs = 3 + n_unmasked + (1 - n_unmasked % 2)

    def job(j):
        u = j - 3
        is_full = u < n_full
        is_sel = jnp.logical_and(u >= n_full, u - n_full < i)
        plane = jnp.where(is_full, P_WIN, jnp.where(is_sel, P_SEL, P_NONE))
        kv = jnp.where(is_full, i - 1 - u, jnp.where(is_sel, u - n_full, 0))
        state = jnp.where(is_full, WIN, SEL)
        plane = jnp.where(j == 0, P_WIN, jnp.where(j == 1, P_SEL, jnp.where(j == 2, P_WIN, plane)))
        kv = jnp.where(j < 2, i, jnp.where(j == 2, jnp.maximum(i - n_back, 0), kv))
        state = jnp.where(j == 0, WIN, jnp.where(j == 1, SEL, jnp.where(j == 2, WIN, state)))
        return state, plane, kv * TK

    def run(it_pv, it_sm, it_qk, par, mask=None):
        if it_qk is not None:
            _, plane, k0 = job(it_qk)
            kt = kcat_ref[0, plane, pl.ds(pl.multiple_of(k0, TK), TK), :]
        if it_pv is not None:
            st_pv, plane, k0 = job(it_pv)
            vT = vcat_ref[0, jnp.minimum(plane, 1), :, pl.ds(pl.multiple_of(k0, TK), TK)]
            vaug = [jnp.concatenate([vT[g * D:(g + 1) * D], ones_rows], axis=0) for g in range(G)]
        if it_sm is not None:
            st_sm, _, _ = job(it_sm)
        for c in range(NSA_HEADS):
            cb = slice(c * TQ, (c + 1) * TQ)
            if it_qk is not None:
                stage_qk(kt, cb, s_bufs[par])
            if it_pv is not None:
                stage_pv(st_pv, vaug[c // NSA_HPG], cb, p_bufs[par], a_bufs[par])
            if it_sm is not None:
                stage_sm(st_sm, cb, s_bufs[1 - par], p_bufs[1 - par], a_bufs[1 - par], mask)

    run(None, None, 0, 0)

    def gate_vec(c):
        return jnp.concatenate([gate_ref[0, gh * 3 + c:gh * 3 + c + 1, :] for gh in range(NSA_HEADS)], axis=1)

    tq_g = t0 + (lax.broadcasted_iota(jnp.int32, (1, GW), 1) & (TQ - 1))
    n_idx = lax.broadcasted_iota(jnp.int32, (nc, GW), 0)
    cmask = (n_idx * CMP_STRIDE + (CMP_BLOCK - 1) <= tq_g) & (n_idx < n_cmp)
    zpad = jnp.zeros((D, GW), BF16)
    groups = range(G)
    gcols = [slice(g * GW, (g + 1) * GW) for g in groups]
    q128 = [jnp.concatenate([qall_ref[g * D:(g + 1) * D, gcols[g]], zpad], axis=0) for g in groups]
    s = [jnp.where(cmask, _dot(kcmp_ref[0, g], q128[g]), NEG) for g in groups]
    m = [jnp.max(s[g], axis=0, keepdims=True) for g in groups]
    e = [jnp.where(cmask, jnp.exp2(s[g] - m[g]), 0.0) for g in groups]
    l = [jnp.sum(e[g], axis=0, keepdims=True) for g in groups]
    p = [e[g] * jnp.where(l[g] > 0.0, 1.0 / l[g], 0.0) for g in groups]
    for g in groups:
        out_ref[:, gcols[g]] = _dot(vcT_ref[0, g], p[g].astype(BF16))
    psum = jnp.concatenate(
        [sum(p[g][:, h * TQ:(h + 1) * TQ] for h in range(1, NSA_HPG)) + p[g][:, 0:TQ] for g in groups], axis=1)
    p_slc = _split_dot(ovl_ref[...], psum)
    tq2 = t0 + (lax.broadcasted_iota(jnp.int32, (1, G * TQ), 1) & (TQ - 1))
    j_idx = lax.broadcasted_iota(jnp.int32, (ns, G * TQ), 0)
    cur = tq2 // SLC_BLOCK
    forced = (j_idx == 0) | (j_idx == cur) | (j_idx == cur - 1)
    score = jnp.where(j_idx <= cur, jnp.where(forced, FORCE_SCORE, p_slc), -jnp.inf)
    ranks = [jnp.zeros((ns, G * TQ), F32), jnp.zeros((ns, G * TQ), F32)]
    for a in range(ns):
        sa = score[a:a + 1, :]
        beats = (sa > score) | ((sa == score) & (j_idx > a))
        ranks[a % 2] = ranks[a % 2] + beats.astype(F32)
    selbias = jnp.where(ranks[0] + ranks[1] < top_k, 0.0, NEG).astype(BF16)
    for g in groups:
        qall_ref[2 * D:2 * D + ns, gcols[g]] = jnp.concatenate([selbias[:, g * TQ:(g + 1) * TQ]] * NSA_HPG, axis=1)
    out_ref[...] = out_ref[...] * gate_vec(0)

    diag = row <= qcol
    tail = (row > qcol) & (i >= n_back)
    run(None, 0, 1, 1, diag)
    run(0, 1, 2, 0, diag)
    run(1, 2, 3, 1, tail)

    def pair(v, carry):
        a = 3 + 2 * v
        run(a - 1, a, a + 1, 0)
        run(a, a + 1, a + 2, 1)
        return carry

    lax.fori_loop(0, (n_jobs - 4) // 2, pair, 0)
    run(n_jobs - 2, n_jobs - 1, None, 0)
    run(n_jobs - 1, None, None, 1)

    def branch_out(state, c):
        return gate_vec(c) * (acc_ref[state, 0:D, :] * (1.0 / acc_ref[state, D:D + 1, :]))

    out = out_ref[...] + branch_out(SEL, 1) + branch_out(WIN, 2)
    stacked = jnp.concatenate([out[:, gh * TQ:(gh + 1) * TQ] for gh in range(NSA_HEADS)], axis=0)
    o_ref[0] = stacked.T.astype(BF16)


def _gla_out_kernel(gq_ref, gk_ref, la_ref, gv_ref, gout_ref, gain_ref, x_ref, on_ref, wo_ref, h_ref,
                    st_ref, *, tg):
    L = GLA_CHUNK

    @pl.when(pl.program_id(1) == 0)
    def _():
        st_ref[...] = jnp.zeros_like(st_ref)

    h_nsa = x_ref[0] + _dot(on_ref[0], wo_ref[0:NSA_Q_W, :])
    la = la_ref[0]
    row = lax.broadcasted_iota(jnp.int32, (tg, tg), 0)
    col = lax.broadcasted_iota(jnp.int32, (tg, tg), 1)
    same = (row // L) == (col // L)
    tril = same & (col <= row)
    n_ch = tg // L
    heads = range(GLA_HEADS)
    chunks = [slice(c * L, (c + 1) * L) for c in range(n_ch)]
    b = _split_dot(tril.astype(BF16), la)
    bl_rows = [b[(c + 1) * L - 1:(c + 1) * L, :] for c in range(n_ch)]
    bl = jnp.concatenate([jnp.broadcast_to(r, (L, GLA_QK_W)) for r in bl_rows], axis=0)
    gk = gk_ref[0]
    qt = (gq_ref[0] * (GLA_DK ** -0.5) * jnp.exp(b))
    kt = (gk * jnp.exp(-b)).astype(BF16)
    kd = (gk * jnp.exp(bl - b)).astype(BF16)
    dec = [jnp.exp(r) for r in bl_rows]
    lane = lax.broadcasted_iota(jnp.int32, (1, GLA_QK_W), 1)
    gain = gain_ref[...]
    qth = [jnp.where((lane // GLA_DK) == h, qt, 0.0).astype(BF16) for h in heads]
    vh = [gv_ref[0, :, h * GLA_DV:(h + 1) * GLA_DV] for h in heads]
    a = [jnp.where(tril, _dot_nt(qth[h], kt), 0.0).astype(BF16) for h in heads]
    vT = [vh[h].astype(F32).T.astype(BF16) for h in heads]
    d_st = [[_dot(vT[h][:, rows], kd[rows]) for rows in chunks] for h in heads]
    o_intra = [_dot(a[h], vh[h]) for h in heads]
    st_in = [[None] * n_ch for _ in heads]
    for h in heads:
        st = st_ref[h]
        for c in range(n_ch):
            st_in[h][c] = st.astype(BF16)
            st = st * dec[c] + d_st[h][c]
        st_ref[h] = st
    outs = []
    for h in heads:
        o = o_intra[h] + jnp.concatenate(
            [_dot_nt(qth[h][rows], st_in[h][c]) for c, rows in enumerate(chunks)], axis=0)
        oms = jnp.mean(o * o, axis=-1, keepdims=True)
        y = o * lax.rsqrt(oms + EPS) * gain
        go = gout_ref[0, :, h * GLA_DV:(h + 1) * GLA_DV].astype(F32)
        outs.append((y * (go * _sigmoid(go))).astype(BF16))
    o_gla = jnp.concatenate(outs, axis=1)
    h_ref[0] = h_nsa + _dot(o_gla, wo_ref[NSA_Q_W:NSA_Q_W + GLA_V_W, :])


def _ffn_kernel(h_ref, halo_ref, g_ref, wup_ref, cw_ref, cb_ref, wdn_ref, o_ref, u_ref, act_ref,
                *, tm, tf, tiles_per_seq):
    h = h_ref[...]
    hh = jnp.concatenate([halo_ref[...], h], axis=0)
    ms = jnp.mean(hh * hh, axis=-1, keepdims=True)
    hn = (hh * lax.rsqrt(ms + EPS) * g_ref[...]).astype(BF16)
    halo_keep = jnp.where((pl.program_id(0) % tiles_per_seq) == 0, 0.0, 1.0)
    lo, hi = SUBLANES, tm + SUBLANES

    def conv(col0, slot):
        u = _dot(hn, wup_ref[:, col0:col0 + tf])
        u_ref[slot, lo:hi, :] = u[lo:hi]
        u_ref[slot, 0:lo, :] = u[0:lo] * halo_keep
        w = cw_ref[:, col0:col0 + tf]
        return (u[lo:hi] * w[2:3] + u_ref[slot, lo - 1:hi - 1, :] * w[1:2]
                + u_ref[slot, lo - 2:hi - 2, :] * w[0:1] + cb_ref[:, col0:col0 + tf])

    for c in range(D_FF // tf):
        gate = conv(c * tf, 2 * (c % 2))
        up = conv(D_FF + c * tf, 2 * (c % 2) + 1)
        act_ref[:, c * tf:(c + 1) * tf] = (gate * _sigmoid(gate) * up).astype(BF16)
    o_ref[...] = h + _dot(act_ref[...], wdn_ref[...])


def _rope_tables(pos):
    inv = ROPE_THETA ** (-np.arange(ROT_HALF, dtype=np.float32) / ROT_HALF)
    ang = pos.astype(np.float32)[:, None] * inv[None, :]
    cos, sin = np.cos(ang), np.sin(ang)
    n = len(pos)
    c = np.ones((n, LANES), np.float32)
    a = np.zeros((n, LANES), np.float32)
    b = np.zeros((n, LANES), np.float32)
    for base in range(0, LANES, NSA_HEAD_DIM):
        c[:, base:base + ROT_HALF] = cos
        c[:, base + ROT_HALF:base + ROT_DIM] = cos
        a[:, base:base + ROT_HALF] = -sin
        b[:, base + ROT_HALF:base + ROT_DIM] = sin
    return cos, sin, c, a, b


def _const_spec(shape):
    nd = len(shape)
    return pl.BlockSpec(shape, lambda *_: (0,) * nd)


def kernel(x, attn_norm, w_in, nsa_q_norm, nsa_k_norm, cmp_pos_k, cmp_pos_v, cmp_w1_k, cmp_w2_k,
           cmp_w1_v, cmp_w2_v, gla_w_gate2, gla_b_gate, gla_out_norm, w_out, ffn_norm, w_up,
           conv_w, conv_b, w_down):
    B, S, DM = x.shape
    depth = attn_norm.shape[0]
    h = x
    for l in range(depth):
        h = _layer(h, attn_norm[l], w_in[l], nsa_q_norm[l], nsa_k_norm[l], cmp_pos_k[l], cmp_pos_v[l],
                   cmp_w1_k[l], cmp_w2_k[l], cmp_w1_v[l], cmp_w2_v[l], gla_w_gate2[l], gla_b_gate[l],
                   gla_out_norm[l], w_out[l], ffn_norm[l], w_up[l], conv_w[l], conv_b[l], w_down[l])
    return h


def _layer(x, attn_norm, w_in, q_gain, k_gains, pe_k, pe_v, w1_k, w2_k, w1_v, w2_v, w_gate2, b_gate,
           out_gain, w_out, ffn_norm, w_up, conv_w, conv_b, w_down):
    B, S, DM = x.shape
    G, D = NSA_KV_GROUPS, NSA_HEAD_DIM
    tm = min(512, S)
    tq = min(256, S)
    tg = min(256, S)
    n_s = S // tm
    nc = S // CMP_STRIDE
    n_cmp = (S - CMP_BLOCK) // CMP_STRIDE + 1
    ns = S // SLC_BLOCK
    top_k = min(SLC_TOPK, ns)
    assert S % tm == 0 and S % tq == 0 and WINDOW % tq == 0 and ns <= LANES and nc % SUBLANES == 0

    offs = np.cumsum([0] + IN_SPLITS)
    nq0, nkv0, ngate0, gq0, gk0, gv0, glr0, gout0 = [int(o) for o in offs[:-1]]
    kvw = NSA_KV_W
    cols = lambda a, n: w_in[:, a:a + n]
    w_feat = jnp.concatenate([cols(nq0, NSA_Q_W), cols(nkv0 + 3 * kvw, kvw), cols(nkv0 + 5 * kvw, kvw),
                              cols(ngate0, 3 * NSA_HEADS),
                              jnp.zeros((DM, GATE_ROWS - 3 * NSA_HEADS), w_in.dtype)], axis=1)
    wf = w_feat.T.astype(BF16)
    w_tok = jnp.concatenate([cols(nkv0 + 2 * kvw, kvw), cols(nkv0 + 4 * kvw, kvw),
                             cols(nkv0, kvw), cols(nkv0 + kvw, kvw),
                             cols(gq0, GLA_QK_W), cols(gk0, GLA_QK_W), cols(gv0, GLA_V_W),
                             cols(gout0, GLA_V_W), cols(glr0, GLA_GATE_RANK),
                             jnp.zeros((DM, LANES - GLA_GATE_RANK), w_in.dtype)], axis=1)
    wt = w_tok.astype(BF16)
    nf, nt = wf.shape[0], wt.shape[1]
    wg2 = jnp.concatenate([w_gate2, jnp.zeros((LANES - GLA_GATE_RANK, GLA_QK_W), w_gate2.dtype)],
                          axis=0).astype(BF16)
    cos, sin, rc, ra, rb = _rope_tables(np.arange(S))
    cosT, sinT = jnp.asarray(cos.T), jnp.asarray(sin.T)
    cmp_end = np.arange(nc) * CMP_STRIDE + CMP_BLOCK - 1
    _, _, cc, ca, cb = _rope_tables(cmp_end)
    bd = np.kron(np.eye(LANES // D, dtype=np.float32), np.full((D, D), 1.0 / D, np.float32))
    tile2 = lambda v: jnp.tile(v.reshape(1, D), (1, LANES // D))

    x2 = x.reshape(B * S, DM)
    tok = lambda w: pl.BlockSpec((tm, w), lambda s, b: (b * n_s + s, 0))
    feat = lambda r: pl.BlockSpec((1, r, tm), lambda s, b: (b, 0, s))
    stab = lambda r, w: pl.BlockSpec((r, w), lambda s, b: (0, s)) if r == ROT_HALF else \
        pl.BlockSpec((r, w), lambda s, b: (s, 0))
    outs = pl.pallas_call(
        _in_proj_kernel,
        grid=(n_s, B),
        in_specs=[tok(DM), _const_spec((1, DM)), _const_spec((nf, DM)), _const_spec((DM, nt)),
                  _const_spec((D, 1)), _const_spec((1, LANES)), _const_spec((1, LANES)),
                  _const_spec((LANES, LANES)),
                  stab(ROT_HALF, tm), stab(ROT_HALF, tm), stab(tm, LANES), stab(tm, LANES), stab(tm, LANES),
                  _const_spec((LANES, GLA_QK_W)), _const_spec((1, GLA_QK_W))],
        out_specs=[feat(NSA_Q_W), pl.BlockSpec((1, 2, kvw, tm), lambda s, b: (b, 0, 0, s)), feat(GATE_ROWS),
                   pl.BlockSpec((1, 3, tm, 2 * LANES), lambda s, b: (b, 0, s, 0)),
                   tok(2 * kvw), tok(GLA_QK_W), tok(GLA_QK_W), tok(GLA_QK_W),
                   tok(GLA_V_W), tok(GLA_V_W)],
        out_shape=[jax.ShapeDtypeStruct((B, NSA_Q_W, S), BF16), jax.ShapeDtypeStruct((B, 2, kvw, S), BF16),
                   jax.ShapeDtypeStruct((B, GATE_ROWS, S), F32),
                   jax.ShapeDtypeStruct((B, 3, S, 2 * LANES), BF16),
                   jax.ShapeDtypeStruct((B * S, 2 * kvw), BF16),
                   jax.ShapeDtypeStruct((B * S, GLA_QK_W), F32), jax.ShapeDtypeStruct((B * S, GLA_QK_W), F32),
                   jax.ShapeDtypeStruct((B * S, GLA_QK_W), F32),
                   jax.ShapeDtypeStruct((B * S, GLA_V_W), BF16), jax.ShapeDtypeStruct((B * S, GLA_V_W), BF16)],
        compiler_params=pltpu.CompilerParams(dimension_semantics=("arbitrary", "arbitrary"),
                                             vmem_limit_bytes=VMEM_LIMIT),
        name="in_proj",
    )(x2, attn_norm.reshape(1, DM), wf, wt, q_gain.reshape(D, 1), tile2(k_gains[1]), tile2(k_gains[2]),
      jnp.asarray(bd, BF16), cosT, sinT, jnp.asarray(rc), jnp.asarray(ra), jnp.asarray(rb),
      wg2, b_gate.reshape(1, GLA_QK_W))
    qT, vcat, gateT, kcat, kvc, gq, gk, la, gv, gout = outs

    r = kvc.reshape(B, nc, CMP_STRIDE, 2 * G, D).transpose(0, 3, 1, 2, 4).reshape(B, 2 * G, nc, CMP_STRIDE * D)
    pad_w2 = lambda w: jnp.concatenate([w, jnp.zeros((CMP_HIDDEN, LANES - D), w.dtype)], axis=1).astype(BF16)
    pe_rows = lambda p: jnp.tile(p.reshape(1, CMP_BLOCK * D), (SUBLANES, 1)).astype(BF16)
    cin = CMP_BLOCK * D
    kcmp, vcT = pl.pallas_call(
        functools.partial(_compress_kernel, nc=nc),
        grid=(B,),
        in_specs=[pl.BlockSpec((1, 2 * G, nc, CMP_STRIDE * D), lambda b: (b, 0, 0, 0)),
                  _const_spec((cin, CMP_HIDDEN)), _const_spec((CMP_HIDDEN, LANES)),
                  _const_spec((cin, CMP_HIDDEN)), _const_spec((CMP_HIDDEN, LANES)),
                  _const_spec((SUBLANES, cin)), _const_spec((SUBLANES, cin)), _const_spec((1, LANES)),
                  _const_spec((nc, LANES)), _const_spec((nc, LANES)), _const_spec((nc, LANES))],
        out_specs=[pl.BlockSpec((1, G, nc, LANES), lambda b: (b, 0, 0, 0)),
                   pl.BlockSpec((1, G, D, nc), lambda b: (b, 0, 0, 0))],
        out_shape=[jax.ShapeDtypeStruct((B, G, nc, LANES), BF16), jax.ShapeDtypeStruct((B, G, D, nc), BF16)],
        compiler_params=pltpu.CompilerParams(dimension_semantics=("arbitrary",), vmem_limit_bytes=VMEM_LIMIT),
        name="compress",
    )(r, w1_k.astype(BF16), pad_w2(w2_k), w1_v.astype(BF16), pad_w2(w2_v), pe_rows(pe_k), pe_rows(pe_v),
      tile2(k_gains[0]), jnp.asarray(cc), jnp.asarray(ca), jnp.asarray(cb))

    ci = np.arange(nc)[None, :]
    sj = np.arange(ns)[:, None]
    ovl = ((ci * CMP_STRIDE < (sj + 1) * SLC_BLOCK) & (ci * CMP_STRIDE + CMP_BLOCK > sj * SLC_BLOCK)
           & (ci < n_cmp)).astype(np.float32)
    n_q = S // tq
    wq = NSA_HEADS * tq
    o_nsa = pl.pallas_call(
        functools.partial(_nsa_kernel, tq_size=tq, nc=nc, n_cmp=n_cmp, ns=ns, top_k=top_k),
        grid=(B, n_q),
        in_specs=[pl.BlockSpec((1, NSA_Q_W, tq), lambda b, i: (b, 0, i)),
                  pl.BlockSpec((1, G, nc, LANES), lambda b, i: (b, 0, 0, 0)),
                  pl.BlockSpec((1, G, D, nc), lambda b, i: (b, 0, 0, 0)),
                  _const_spec((ns, nc)),
                  pl.BlockSpec((1, 3, S, 2 * LANES), lambda b, i: (b, 0, 0, 0)),
                  pl.BlockSpec((1, 2, kvw, S), lambda b, i: (b, 0, 0, 0)),
                  pl.BlockSpec((1, GATE_ROWS, tq), lambda b, i: (b, 0, i))],
        out_specs=pl.BlockSpec((1, tq, NSA_Q_W), lambda b, i: (b, i, 0)),
        out_shape=jax.ShapeDtypeStruct((B, S, NSA_Q_W), BF16),
        scratch_shapes=[pltpu.VMEM((2 * D + LANES, wq), BF16),
                        pltpu.VMEM((tq, wq), F32), pltpu.VMEM((tq, wq), F32),
                        pltpu.VMEM((tq, wq), BF16), pltpu.VMEM((tq, wq), BF16),
                        pltpu.VMEM((1, wq), F32), pltpu.VMEM((1, wq), F32),
                        pltpu.VMEM((2, D + 2 * SUBLANES, wq), F32),
                        pltpu.VMEM((2, 1, wq), F32),
                        pltpu.VMEM((D, wq), F32)],
        compiler_params=pltpu.CompilerParams(dimension_semantics=("arbitrary", "arbitrary"),
                                             vmem_limit_bytes=VMEM_LIMIT),
        name="nsa_attn",
    )(qT, kcmp, vcT, jnp.asarray(ovl, BF16), kcat, vcat, gateT)

    g3 = lambda a: a.reshape(B, S, a.shape[-1])
    gspec = lambda w: pl.BlockSpec((1, tg, w), lambda b, t: (b, t, 0))
    hmid = pl.pallas_call(
        functools.partial(_gla_out_kernel, tg=tg),
        grid=(B, S // tg),
        in_specs=[gspec(GLA_QK_W), gspec(GLA_QK_W), gspec(GLA_QK_W), gspec(GLA_V_W), gspec(GLA_V_W),
                  _const_spec((1, GLA_DV)), gspec(DM), gspec(NSA_Q_W), _const_spec((NSA_Q_W + GLA_V_W, DM))],
        out_specs=gspec(DM),
        out_shape=jax.ShapeDtypeStruct((B, S, DM), F32),
        scratch_shapes=[pltpu.VMEM((GLA_HEADS, GLA_DV, GLA_QK_W), F32)],
        compiler_params=pltpu.CompilerParams(dimension_semantics=("arbitrary", "arbitrary"),
                                             vmem_limit_bytes=VMEM_LIMIT),
        name="gla_out",
    )(g3(gq), g3(gk), g3(la), g3(gv), g3(gout), out_gain.reshape(1, GLA_DV), x, o_nsa, w_out.astype(BF16))
    hmid = hmid.reshape(B * S, DM)
    n_t = (B * S) // tm
    row = lambda w: pl.BlockSpec((tm, w), lambda t: (t, 0))

    tf = 256
    assert D_FF % tf == 0
    halo_blocks = tm // SUBLANES
    out = pl.pallas_call(
        functools.partial(_ffn_kernel, tm=tm, tf=tf, tiles_per_seq=n_s),
        grid=(n_t,),
        in_specs=[row(DM),
                  pl.BlockSpec((SUBLANES, DM), lambda t: (jnp.maximum(t * halo_blocks - 1, 0), 0)),
                  _const_spec((1, DM)),
                  pl.BlockSpec((DM, 2 * D_FF), lambda t: (0, 0), pipeline_mode=pl.Buffered(1)),
                  _const_spec((CONV_W, 2 * D_FF)), _const_spec((1, 2 * D_FF)),
                  pl.BlockSpec((D_FF, DM), lambda t: (0, 0), pipeline_mode=pl.Buffered(1))],
        out_specs=row(DM),
        out_shape=jax.ShapeDtypeStruct((B * S, DM), F32),
        scratch_shapes=[pltpu.VMEM((4, tm + SUBLANES, tf), F32), pltpu.VMEM((tm, D_FF), BF16)],
        compiler_params=pltpu.CompilerParams(dimension_semantics=("arbitrary",), vmem_limit_bytes=VMEM_LIMIT),
        name="ffn",
    )(hmid, hmid, ffn_norm.reshape(1, DM), w_up.astype(BF16), conv_w, conv_b.reshape(1, 2 * D_FF),
      w_down.astype(BF16))
    return out.reshape(B, S, DM)
```

```python
import functools

import numpy as np
import jax
import jax.numpy as jnp
from jax import lax
from jax.experimental import pallas as pl
from jax.experimental.pallas import tpu as pltpu

F32 = jnp.float32
BF16 = jnp.bfloat16

NSA_HEADS = 8
NSA_KV_GROUPS = 2
NSA_HEAD_DIM = 64
NSA_HPG = NSA_HEADS // NSA_KV_GROUPS
CMP_BLOCK = 32
CMP_STRIDE = 16
CMP_HIDDEN = 256
SLC_BLOCK = 64
SLC_TOPK = 8
WINDOW = 512
FORCE_SCORE = 1e4
GLA_HEADS = 4
GLA_DK = 64
GLA_DV = 128
GLA_GATE_RANK = 16
GLA_TAU = 16.0
GLA_CHUNK = 64
ROPE_THETA = 500000.0
ROT_DIM = NSA_HEAD_DIM // 4
ROT_HALF = ROT_DIM // 2
D_FF = 2816
CONV_W = 3
EPS = 1e-6
NEG = -1e30
LOG2E = 1.4426950408889634

NSA_Q_W = NSA_HEADS * NSA_HEAD_DIM
NSA_KV_W = NSA_KV_GROUPS * NSA_HEAD_DIM
GLA_QK_W = GLA_HEADS * GLA_DK
GLA_V_W = GLA_HEADS * GLA_DV
IN_SPLITS = [NSA_Q_W, 6 * NSA_KV_W, 3 * NSA_HEADS, GLA_QK_W, GLA_QK_W, GLA_V_W, GLA_GATE_RANK, GLA_V_W]

LANES = 128
SUBLANES = 8
GATE_ROWS = 32
SCORE_AHEAD = 1
VMEM_LIMIT = 56 * 1024 * 1024


def _dot(a, b):
    return jnp.dot(a, b, preferred_element_type=F32)


def _dot_nt(a, b):
    return lax.dot_general(a, b, (((1,), (1,)), ((), ())), preferred_element_type=F32)


def _split_dot(m_bf16, x_f32):
    hi = x_f32.astype(BF16)
    lo = (x_f32 - hi.astype(F32)).astype(BF16)
    return _dot(m_bf16, hi) + _dot(m_bf16, lo)


def _split_dot_r(x_f32, m_bf16):
    hi = x_f32.astype(BF16)
    lo = (x_f32 - hi.astype(F32)).astype(BF16)
    return _dot(hi, m_bf16) + _dot(lo, m_bf16)


def _rope_lanes(y, c, a, b):
    return y * c + pltpu.roll(y, LANES - ROT_HALF, 1) * a + pltpu.roll(y, ROT_HALF, 1) * b


def _sigmoid(z):
    return 1.0 / (1.0 + jnp.exp(-z))


def _in_proj_kernel(x_ref, g_ref, wf_ref, wt_ref, qg_ref, kgs_ref, kgw_ref, bd_ref,
                    cosT_ref, sinT_ref, rc_ref, ra_ref, rb_ref, wg2_ref, bg_ref,
                    qT_ref, vcat_ref, gateT_ref, kcat_ref, kc_ref, vc_ref,
                    gq_ref, gk_ref, la_ref, gv_ref, gout_ref):
    tm = x_ref.shape[0]
    x = x_ref[...]
    ms = jnp.mean(x * x, axis=-1, keepdims=True)
    xn = (x * lax.rsqrt(ms + EPS) * g_ref[...]).astype(BF16)

    offs = np.cumsum([0, NSA_KV_W, NSA_KV_W, 2 * NSA_KV_W, GLA_QK_W, GLA_QK_W, GLA_V_W, GLA_V_W, LANES])
    c_ks, c_kw, c_kvc, c_gq, c_gk, c_gv, c_gout, c_glr, c_end = [int(v) for v in offs]

    def seg(a, b):
        return _dot(xn, wt_ref[:, a:b])

    zf = _dot_nt(wf_ref[...], xn)
    z_ks = seg(c_ks, c_kw)
    z_kw = seg(c_kw, c_kvc)
    cosT = cosT_ref[...]
    sinT = sinT_ref[...]
    qg = qg_ref[...]
    scale = NSA_HEAD_DIM ** -0.5 * LOG2E

    def q_head(h):
        blk = zf[h * NSA_HEAD_DIM:(h + 1) * NSA_HEAD_DIM]
        hms = jnp.mean(blk * blk, axis=0, keepdims=True)
        y = blk * lax.rsqrt(hms + EPS) * qg
        y1 = y[0:ROT_HALF]
        y2 = y[ROT_HALF:ROT_DIM]
        r = jnp.concatenate([y1 * cosT - y2 * sinT, y2 * cosT + y1 * sinT, y[ROT_DIM:]], axis=0)
        qT_ref[0, h * NSA_HEAD_DIM:(h + 1) * NSA_HEAD_DIM, :] = (r * scale).astype(BF16)

    gq_ref[...] = seg(c_gq, c_gk)
    for h in range(0, NSA_HEADS // 2):
        q_head(h)
    gk_ref[...] = seg(c_gk, c_gv)
    for h in range(NSA_HEADS // 2, NSA_HEADS):
        q_head(h)
    gv_ref[...] = seg(c_gv, c_gout).astype(BF16)
    o = NSA_Q_W
    vcat_ref[0, 0] = zf[o:o + NSA_KV_W].astype(BF16)
    vcat_ref[0, 1] = zf[o + NSA_KV_W:o + 2 * NSA_KV_W].astype(BF16)
    gateT_ref[0] = _sigmoid(zf[o + 2 * NSA_KV_W:o + 2 * NSA_KV_W + GATE_ROWS])

    bd = bd_ref[...]
    rc, ra, rb = rc_ref[...], ra_ref[...], rb_ref[...]

    def knorm(z, gain):
        gms = _split_dot_r(z * z, bd)
        return _rope_lanes(z * lax.rsqrt(gms + EPS) * gain, rc, ra, rb).astype(BF16)

    pos = pl.program_id(0) * tm + lax.broadcasted_iota(jnp.int32, (tm, LANES), 0)
    lane = lax.broadcasted_iota(jnp.int32, (tm, LANES), 1)
    onehot = jnp.where(pos // SLC_BLOCK == lane, 1.0, 0.0).astype(BF16)
    zl = jnp.zeros((tm, LANES), BF16)
    gout_ref[...] = seg(c_gout, c_glr).astype(BF16)
    kcat_ref[0, 0] = jnp.concatenate([knorm(z_ks, kgs_ref[...]), onehot], axis=1)
    kvc = seg(c_kvc, c_gq)
    kc_ref[...] = kvc[:, 0:NSA_KV_W]
    vc_ref[...] = kvc[:, NSA_KV_W:2 * NSA_KV_W]
    kcat_ref[0, 1] = jnp.concatenate([knorm(z_kw, kgw_ref[...]), zl], axis=1)
    glr = seg(c_glr, c_end).astype(BF16)
    pre = _dot(glr, wg2_ref[...]) + bg_ref[...]
    la_ref[...] = (jnp.minimum(pre, 0.0) - jnp.log1p(jnp.exp(-jnp.abs(pre)))) * (1.0 / GLA_TAU)


def _gelu_tanh(x):
    return 0.5 * x * (1.0 + jnp.tanh(0.7978845608028654 * (x + 0.044715 * (x * x * x))))


def _compress_kernel(kc_ref, vc_ref, w1k_ref, w2k_ref, w1v_ref, w2v_ref, pek_ref, pev_ref, kg_ref,
                     cc_ref, ca_ref, cb_ref, kcmp_ref, vcT_ref, *, nc):
    half = CMP_STRIDE * NSA_HEAD_DIM
    taps = [[src[0, pl.ds(t, nc, stride=CMP_STRIDE), :] for t in range(CMP_STRIDE)] for src in (kc_ref, vc_ref)]
    for j in range(2 * NSA_KV_GROUPS):
        is_k = j < NSA_KV_GROUPS
        g = j % NSA_KV_GROUPS
        w1 = w1k_ref if is_k else w1v_ref
        w2 = w2k_ref if is_k else w2v_ref
        pe = pek_ref if is_k else pev_ref
        lanes = slice(g * NSA_HEAD_DIM, (g + 1) * NSA_HEAD_DIM)
        r = jnp.concatenate([tp[:, lanes] for tp in taps[0 if is_k else 1]],
                            axis=1).astype(BF16)
        p1 = _dot(r, w1[0:half, :])
        p2 = _dot(r, w1[half:2 * half, :])
        cst = _dot(pe[...], w1[...])[0:1]
        hid = p1 + pltpu.roll(p2, nc - 1, 0) + cst
        out = _dot(_gelu_tanh(hid).astype(BF16), w2[...])
        if is_k:
            kms = jnp.sum(out * out, axis=-1, keepdims=True) * (1.0 / NSA_HEAD_DIM)
            y = out * lax.rsqrt(kms + EPS) * kg_ref[...]
            kcmp_ref[0, g] = _rope_lanes(y, cc_ref[...], ca_ref[...], cb_ref[...]).astype(BF16)
        else:
            vcT_ref[0, g] = out.T[0:NSA_HEAD_DIM].astype(BF16)


def _nsa_kernel(qT_ref, kcmp_ref, vcT_ref, ovl_ref, kcat_ref, vcat_ref, gate_ref, o_ref,
                qall_ref, p_ref, al_ref, acc_ref, m_ref, out_ref,
                *, tq_size, nc, n_cmp, ns, top_k):
    TQ = tq_size
    TK = tq_size
    D = NSA_HEAD_DIM
    G = NSA_KV_GROUPS
    GW = NSA_HPG * TQ
    W = NSA_HEADS * TQ
    WIN, SEL = 0, 1
    P_SEL, P_WIN = 0, 1
    n_back = WINDOW // TK
    i = pl.program_id(1)
    t0 = i * TQ

    zeros = jnp.zeros((D, TQ), BF16)
    for gh in range(NSA_HEADS):
        g = gh // NSA_HPG
        cols = slice(gh * TQ, (gh + 1) * TQ)
        qall_ref[g * D:(g + 1) * D, cols] = qT_ref[0, gh * D:(gh + 1) * D, :]
        qall_ref[(1 - g) * D:(2 - g) * D, cols] = zeros
    qall_ref[2 * D:2 * D + LANES, :] = jnp.zeros((LANES, W), BF16)
    m_ref[...] = jnp.full(m_ref.shape, NEG, F32)
    acc_ref[...] = jnp.zeros(acc_ref.shape, F32)

    row = lax.broadcasted_iota(jnp.int32, (TK, TQ), 0)
    qcol = lax.broadcasted_iota(jnp.int32, (TK, TQ), 1)
    ones_rows = jnp.ones((2 * SUBLANES, TK), BF16)

    n_full = jnp.minimum(i, n_back - 1)
    last = 2 + n_full + i

    def job(j):
        u = j - 3
        is_full = u < n_full
        plane = jnp.where(is_full, P_WIN, P_SEL)
        kv = jnp.where(is_full, i - 1 - u, u - n_full)
        state = jnp.where(is_full, WIN, SEL)
        plane = jnp.where(j == 0, P_WIN, jnp.where(j == 1, P_SEL, jnp.where(j == 2, P_WIN, plane)))
        kv = jnp.where(j < 2, i, jnp.where(j == 2, jnp.maximum(i - n_back, 0), kv))
        state = jnp.where(j == 0, WIN, jnp.where(j == 1, SEL, jnp.where(j == 2, WIN, state)))
        return state, plane, pl.multiple_of(kv * TK, TK)

    def k_tile(j):
        _, plane, k0 = job(j)
        return kcat_ref[0, plane, pl.ds(k0, TK), :]

    def v_aug(j):
        _, plane, k0 = job(j)
        vT = vcat_ref[0, plane, :, pl.ds(k0, TK)]
        return [jnp.concatenate([vT[g * D:(g + 1) * D], ones_rows], axis=0) for g in range(G)]

    def head_cols(c):
        return slice(c * TQ, (c + 1) * TQ)

    def pv_block(state, vaug, c):
        cb = head_cols(c)
        acc_ref[state, :, cb] = al_ref[:, cb] * acc_ref[state, :, cb] + _dot(vaug[c // NSA_HPG], p_ref[:, cb])

    def sm_block(state, c, s_blk, mask):
        for hb in range(TQ // LANES):
            lanes = slice(hb * LANES, (hb + 1) * LANES)
            cb = slice(c * TQ + hb * LANES, c * TQ + (hb + 1) * LANES)
            s = s_blk[:, lanes]
            if mask is not None:
                s = jnp.where(mask[:, lanes], s, NEG)
            m_old = m_ref[state, :, cb]
            m_new = jnp.maximum(m_old, jnp.max(s, axis=0, keepdims=True))
            al_ref[:, cb] = jnp.exp2(m_old - m_new)
            m_ref[state, :, cb] = m_new
            p_ref[:, cb] = jnp.exp2(s - m_new).astype(BF16)

    def scores(kt, c):
        return _dot(kt, qall_ref[:, head_cols(c)])

    def step(j, s_first, j_next, prev, mask=None):
        state, _, _ = job(j)
        kt = k_tile(j)
        kt_next = k_tile(j_next)
        if prev is not None:
            state_prev, _, _ = job(prev)
            vaug = v_aug(prev)
        pending = list(s_first)
        for c in range(NSA_HEADS):
            s_cur = pending.pop(0)
            ahead = c + SCORE_AHEAD
            pending.append(scores(kt, ahead) if ahead < NSA_HEADS else scores(kt_next, ahead - NSA_HEADS))
            if prev is not None:
                pv_block(state_prev, vaug, c)
            sm_block(state, c, s_cur, mask)
        return tuple(pending)

    kt0 = k_tile(0)
    s_first = tuple(scores(kt0, c) for c in range(SCORE_AHEAD))

    def gate_vec(c):
        return jnp.concatenate([gate_ref[0, gh * 3 + c:gh * 3 + c + 1, :] for gh in range(NSA_HEADS)], axis=1)

    tq_g = t0 + (lax.broadcasted_iota(jnp.int32, (1, GW), 1) & (TQ - 1))
    n_idx = lax.broadcasted_iota(jnp.int32, (nc, GW), 0)
    cmask = (n_idx * CMP_STRIDE + (CMP_BLOCK - 1) <= tq_g) & (n_idx < n_cmp)
    zpad = jnp.zeros((D, GW), BF16)
    groups = range(G)
    gcols = [slice(g * GW, (g + 1) * GW) for g in groups]
    q128 = [jnp.concatenate([qall_ref[g * D:(g + 1) * D, gcols[g]], zpad], axis=0) for g in groups]
    s = [jnp.where(cmask, _dot(kcmp_ref[0, g], q128[g]), NEG) for g in groups]
    m = [jnp.max(s[g], axis=0, keepdims=True) for g in groups]
    e = [jnp.where(cmask, jnp.exp2(s[g] - m[g]), 0.0) for g in groups]
    l = [jnp.sum(e[g], axis=0, keepdims=True) for g in groups]
    p = [e[g] * jnp.where(l[g] > 0.0, 1.0 / l[g], 0.0) for g in groups]
    for g in groups:
        out_ref[:, gcols[g]] = _dot(vcT_ref[0, g], p[g].astype(BF16))
    psum = jnp.concatenate(
        [sum(p[g][:, h * TQ:(h + 1) * TQ] for h in range(1, NSA_HPG)) + p[g][:, 0:TQ] for g in groups], axis=1)
    p_slc = _split_dot(ovl_ref[...], psum)
    tq2 = t0 + (lax.broadcasted_iota(jnp.int32, (1, G * TQ), 1) & (TQ - 1))
    j_idx = lax.broadcasted_iota(jnp.int32, (ns, G * TQ), 0)
    cur = tq2 // SLC_BLOCK
    forced = (j_idx == 0) | (j_idx == cur) | (j_idx == cur - 1)
    score = jnp.where(j_idx <= cur, jnp.where(forced, FORCE_SCORE, p_slc), -jnp.inf)
    ranks = [jnp.zeros((ns, G * TQ), F32), jnp.zeros((ns, G * TQ), F32)]
    for a in range(ns):
        sa = score[a:a + 1, :]
        beats = (sa > score) | ((sa == score) & (j_idx > a))
        ranks[a % 2] = ranks[a % 2] + beats.astype(F32)
    selbias = jnp.where(ranks[0] + ranks[1] < top_k, 0.0, NEG).astype(BF16)
    for g in groups:
        qall_ref[2 * D:2 * D + ns, gcols[g]] = jnp.concatenate([selbias[:, g * TQ:(g + 1) * TQ]] * NSA_HPG, axis=1)
    out_ref[...] = out_ref[...] * gate_vec(0)

    diag = row <= qcol
    tail = (row > qcol) & (i >= n_back)
    s_first = step(0, s_first, 1, None, diag)
    s_first = step(1, s_first, 2, 0, diag)
    s_first = step(2, s_first, jnp.minimum(3, last), 1, tail)

    def unmasked(u, s_in):
        j = 3 + u
        return step(j, s_in, jnp.minimum(j + 1, last), j - 1)

    lax.fori_loop(0, last - 2, unmasked, s_first)
    state_last, _, _ = job(last)
    vaug_last = v_aug(last)
    for c in range(NSA_HEADS):
        pv_block(state_last, vaug_last, c)

    def branch_out(state, c):
        return gate_vec(c) * (acc_ref[state, 0:D, :] * (1.0 / acc_ref[state, D:D + 1, :]))

    out = out_ref[...] + branch_out(SEL, 1) + branch_out(WIN, 2)
    stacked = jnp.concatenate([out[:, gh * TQ:(gh + 1) * TQ] for gh in range(NSA_HEADS)], axis=0)
    o_ref[0] = stacked.T.astype(BF16)


def _gla_out_kernel(gq_ref, gk_ref, la_ref, gv_ref, gout_ref, gain_ref, x_ref, on_ref, wo_ref, h_ref,
                    st_ref, *, tg):
    L = GLA_CHUNK

    @pl.when(pl.program_id(1) == 0)
    def _():
        st_ref[...] = jnp.zeros_like(st_ref)

    h_nsa = x_ref[0] + _dot(on_ref[0], wo_ref[0:NSA_Q_W, :])
    la = la_ref[0]
    row = lax.broadcasted_iota(jnp.int32, (tg, tg), 0)
    col = lax.broadcasted_iota(jnp.int32, (tg, tg), 1)
    same = (row // L) == (col // L)
    tril = same & (col <= row)
    n_ch = tg // L
    heads = range(GLA_HEADS)
    chunks = [slice(c * L, (c + 1) * L) for c in range(n_ch)]
    b = _split_dot(tril.astype(BF16), la)
    bl_rows = [b[(c + 1) * L - 1:(c + 1) * L, :] for c in range(n_ch)]
    bl = jnp.concatenate([jnp.broadcast_to(r, (L, GLA_QK_W)) for r in bl_rows], axis=0)
    gk = gk_ref[0]
    qt = (gq_ref[0] * (GLA_DK ** -0.5) * jnp.exp(b))
    kt = (gk * jnp.exp(-b)).astype(BF16)
    kd = (gk * jnp.exp(bl - b)).astype(BF16)
    dec = [jnp.exp(r) for r in bl_rows]
    lane = lax.broadcasted_iota(jnp.int32, (1, GLA_QK_W), 1)
    gain = gain_ref[...]
    qth = [jnp.where((lane // GLA_DK) == h, qt, 0.0).astype(BF16) for h in heads]
    vh = [gv_ref[0, :, h * GLA_DV:(h + 1) * GLA_DV] for h in heads]
    a = [jnp.where(tril, _dot_nt(qth[h], kt), 0.0).astype(BF16) for h in heads]
    vT = [vh[h].astype(F32).T.astype(BF16) for h in heads]
    d_st = [[_dot(vT[h][:, rows], kd[rows]) for rows in chunks] for h in heads]
    o_intra = [_dot(a[h], vh[h]) for h in heads]
    st_in = [[None] * n_ch for _ in heads]
    for h in heads:
        st = st_ref[h]
        for c in range(n_ch):
            st_in[h][c] = st.astype(BF16)
            st = st * dec[c] + d_st[h][c]
        st_ref[h] = st
    outs = []
    for h in heads:
        o = o_intra[h] + jnp.concatenate(
            [_dot_nt(qth[h][rows], st_in[h][c]) for c, rows in enumerate(chunks)], axis=0)
        oms = jnp.mean(o * o, axis=-1, keepdims=True)
        y = o * lax.rsqrt(oms + EPS) * gain
        go = gout_ref[0, :, h * GLA_DV:(h + 1) * GLA_DV].astype(F32)
        outs.append((y * (go * _sigmoid(go))).astype(BF16))
    o_gla = jnp.concatenate(outs, axis=1)
    h_ref[0] = h_nsa + _dot(o_gla, wo_ref[NSA_Q_W:NSA_Q_W + GLA_V_W, :])


def _ffn_kernel(h_ref, halo_ref, g_ref, wup_ref, cw_ref, cb_ref, wdn_ref, o_ref, u_ref, act_ref,
                *, tm, tf, tiles_per_seq):
    h = h_ref[...]
    hh = jnp.concatenate([halo_ref[...], h], axis=0)
    ms = jnp.mean(hh * hh, axis=-1, keepdims=True)
    hn = (hh * lax.rsqrt(ms + EPS) * g_ref[...]).astype(BF16)
    halo_keep = jnp.where((pl.program_id(0) % tiles_per_seq) == 0, 0.0, 1.0)
    lo, hi = SUBLANES, tm + SUBLANES

    def conv(col0, slot):
        u = _dot(hn, wup_ref[:, col0:col0 + tf])
        u_ref[slot, lo:hi, :] = u[lo:hi]
        u_ref[slot, 0:lo, :] = u[0:lo] * halo_keep
        w = cw_ref[:, col0:col0 + tf]
        return (u[lo:hi] * w[2:3] + u_ref[slot, lo - 1:hi - 1, :] * w[1:2]
                + u_ref[slot, lo - 2:hi - 2, :] * w[0:1] + cb_ref[:, col0:col0 + tf])

    for c in range(D_FF // tf):
        gate = conv(c * tf, 2 * (c % 2))
        up = conv(D_FF + c * tf, 2 * (c % 2) + 1)
        act_ref[:, c * tf:(c + 1) * tf] = (gate * _sigmoid(gate) * up).astype(BF16)
    o_ref[...] = h + _dot(act_ref[...], wdn_ref[...])


def _rope_tables(pos):
    inv = ROPE_THETA ** (-np.arange(ROT_HALF, dtype=np.float32) / ROT_HALF)
    ang = pos.astype(np.float32)[:, None] * inv[None, :]
    cos, sin = np.cos(ang), np.sin(ang)
    n = len(pos)
    c = np.ones((n, LANES), np.float32)
    a = np.zeros((n, LANES), np.float32)
    b = np.zeros((n, LANES), np.float32)
    for base in range(0, LANES, NSA_HEAD_DIM):
        c[:, base:base + ROT_HALF] = cos
        c[:, base + ROT_HALF:base + ROT_DIM] = cos
        a[:, base:base + ROT_HALF] = -sin
        b[:, base + ROT_HALF:base + ROT_DIM] = sin
    return cos, sin, c, a, b


def _const_spec(shape):
    nd = len(shape)
    return pl.BlockSpec(shape, lambda *_: (0,) * nd)


def kernel(x, attn_norm, w_in, nsa_q_norm, nsa_k_norm, cmp_pos_k, cmp_pos_v, cmp_w1_k, cmp_w2_k,
           cmp_w1_v, cmp_w2_v, gla_w_gate2, gla_b_gate, gla_out_norm, w_out, ffn_norm, w_up,
           conv_w, conv_b, w_down):
    B, S, DM = x.shape
    depth = attn_norm.shape[0]
    h = x
    for l in range(depth):
        h = _layer(h, attn_norm[l], w_in[l], nsa_q_norm[l], nsa_k_norm[l], cmp_pos_k[l], cmp_pos_v[l],
                   cmp_w1_k[l], cmp_w2_k[l], cmp_w1_v[l], cmp_w2_v[l], gla_w_gate2[l], gla_b_gate[l],
                   gla_out_norm[l], w_out[l], ffn_norm[l], w_up[l], conv_w[l], conv_b[l], w_down[l])
    return h


def _layer(x, attn_norm, w_in, q_gain, k_gains, pe_k, pe_v, w1_k, w2_k, w1_v, w2_v, w_gate2, b_gate,
           out_gain, w_out, ffn_norm, w_up, conv_w, conv_b, w_down):
    B, S, DM = x.shape
    G, D = NSA_KV_GROUPS, NSA_HEAD_DIM
    tm = min(512, S)
    tq = min(256, S)
    tg = min(256, S)
    n_s = S // tm
    nc = S // CMP_STRIDE
    n_cmp = (S - CMP_BLOCK) // CMP_STRIDE + 1
    ns = S // SLC_BLOCK
    top_k = min(SLC_TOPK, ns)
    assert S % tm == 0 and S % tq == 0 and WINDOW % tq == 0 and ns <= LANES and nc % SUBLANES == 0

    offs = np.cumsum([0] + IN_SPLITS)
    nq0, nkv0, ngate0, gq0, gk0, gv0, glr0, gout0 = [int(o) for o in offs[:-1]]
    kvw = NSA_KV_W
    cols = lambda a, n: w_in[:, a:a + n]
    w_feat = jnp.concatenate([cols(nq0, NSA_Q_W), cols(nkv0 + 3 * kvw, kvw), cols(nkv0 + 5 * kvw, kvw),
                              cols(ngate0, 3 * NSA_HEADS),
                              jnp.zeros((DM, GATE_ROWS - 3 * NSA_HEADS), w_in.dtype)], axis=1)
    wf = w_feat.T.astype(BF16)
    w_tok = jnp.concatenate([cols(nkv0 + 2 * kvw, kvw), cols(nkv0 + 4 * kvw, kvw),
                             cols(nkv0, kvw), cols(nkv0 + kvw, kvw),
                             cols(gq0, GLA_QK_W), cols(gk0, GLA_QK_W), cols(gv0, GLA_V_W),
                             cols(gout0, GLA_V_W), cols(glr0, GLA_GATE_RANK),
                             jnp.zeros((DM, LANES - GLA_GATE_RANK), w_in.dtype)], axis=1)
    wt = w_tok.astype(BF16)
    nf, nt = wf.shape[0], wt.shape[1]
    wg2 = jnp.concatenate([w_gate2, jnp.zeros((LANES - GLA_GATE_RANK, GLA_QK_W), w_gate2.dtype)],
                          axis=0).astype(BF16)
    cos, sin, rc, ra, rb = _rope_tables(np.arange(S))
    cosT, sinT = jnp.asarray(cos.T), jnp.asarray(sin.T)
    cmp_end = np.arange(nc) * CMP_STRIDE + CMP_BLOCK - 1
    _, _, cc, ca, cb = _rope_tables(cmp_end)
    bd = np.kron(np.eye(LANES // D, dtype=np.float32), np.full((D, D), 1.0 / D, np.float32))
    tile2 = lambda v: jnp.tile(v.reshape(1, D), (1, LANES // D))

    x2 = x.reshape(B * S, DM)
    tok = lambda w: pl.BlockSpec((tm, w), lambda s, b: (b * n_s + s, 0))
    feat = lambda r: pl.BlockSpec((1, r, tm), lambda s, b: (b, 0, s))
    stab = lambda r, w: pl.BlockSpec((r, w), lambda s, b: (0, s)) if r == ROT_HALF else \
        pl.BlockSpec((r, w), lambda s, b: (s, 0))
    outs = pl.pallas_call(
        _in_proj_kernel,
        grid=(n_s, B),
        in_specs=[tok(DM), _const_spec((1, DM)), _const_spec((nf, DM)), _const_spec((DM, nt)),
                  _const_spec((D, 1)), _const_spec((1, LANES)), _const_spec((1, LANES)),
                  _const_spec((LANES, LANES)),
                  stab(ROT_HALF, tm), stab(ROT_HALF, tm), stab(tm, LANES), stab(tm, LANES), stab(tm, LANES),
                  _const_spec((LANES, GLA_QK_W)), _const_spec((1, GLA_QK_W))],
        out_specs=[feat(NSA_Q_W), pl.BlockSpec((1, 2, kvw, tm), lambda s, b: (b, 0, 0, s)), feat(GATE_ROWS),
                   pl.BlockSpec((1, 2, tm, 2 * LANES), lambda s, b: (b, 0, s, 0)),
                   tok(kvw), tok(kvw), tok(GLA_QK_W), tok(GLA_QK_W), tok(GLA_QK_W),
                   tok(GLA_V_W), tok(GLA_V_W)],
        out_shape=[jax.ShapeDtypeStruct((B, NSA_Q_W, S), BF16), jax.ShapeDtypeStruct((B, 2, kvw, S), BF16),
                   jax.ShapeDtypeStruct((B, GATE_ROWS, S), F32),
                   jax.ShapeDtypeStruct((B, 2, S, 2 * LANES), BF16),
                   jax.ShapeDtypeStruct((B * S, kvw), F32), jax.ShapeDtypeStruct((B * S, kvw), F32),
                   jax.ShapeDtypeStruct((B * S, GLA_QK_W), F32), jax.ShapeDtypeStruct((B * S, GLA_QK_W), F32),
                   jax.ShapeDtypeStruct((B * S, GLA_QK_W), F32),
                   jax.ShapeDtypeStruct((B * S, GLA_V_W), BF16), jax.ShapeDtypeStruct((B * S, GLA_V_W), BF16)],
        compiler_params=pltpu.CompilerParams(dimension_semantics=("arbitrary", "arbitrary"),
                                             vmem_limit_bytes=VMEM_LIMIT),
        name="in_proj",
    )(x2, attn_norm.reshape(1, DM), wf, wt, q_gain.reshape(D, 1), tile2(k_gains[1]), tile2(k_gains[2]),
      jnp.asarray(bd, BF16), cosT, sinT, jnp.asarray(rc), jnp.asarray(ra), jnp.asarray(rb),
      wg2, b_gate.reshape(1, GLA_QK_W))
    qT, vcat, gateT, kcat, kc, vc, gq, gk, la, gv, gout = outs

    seq = lambda a: a.reshape(B, S, kvw)
    pad_w2 = lambda w: jnp.concatenate([w, jnp.zeros((CMP_HIDDEN, LANES - D), w.dtype)], axis=1).astype(BF16)
    pe_rows = lambda p: jnp.tile(p.reshape(1, CMP_BLOCK * D), (SUBLANES, 1)).astype(BF16)
    cin = CMP_BLOCK * D
    kcmp, vcT = pl.pallas_call(
        functools.partial(_compress_kernel, nc=nc),
        grid=(B,),
        in_specs=[pl.BlockSpec((1, S, kvw), lambda b: (b, 0, 0)), pl.BlockSpec((1, S, kvw), lambda b: (b, 0, 0)),
                  _const_spec((cin, CMP_HIDDEN)), _const_spec((CMP_HIDDEN, LANES)),
                  _const_spec((cin, CMP_HIDDEN)), _const_spec((CMP_HIDDEN, LANES)),
                  _const_spec((SUBLANES, cin)), _const_spec((SUBLANES, cin)), _const_spec((1, LANES)),
                  _const_spec((nc, LANES)), _const_spec((nc, LANES)), _const_spec((nc, LANES))],
        out_specs=[pl.BlockSpec((1, G, nc, LANES), lambda b: (b, 0, 0, 0)),
                   pl.BlockSpec((1, G, D, nc), lambda b: (b, 0, 0, 0))],
        out_shape=[jax.ShapeDtypeStruct((B, G, nc, LANES), BF16), jax.ShapeDtypeStruct((B, G, D, nc), BF16)],
        compiler_params=pltpu.CompilerParams(dimension_semantics=("arbitrary",), vmem_limit_bytes=VMEM_LIMIT),
        name="compress",
    )(seq(kc), seq(vc), w1_k.astype(BF16), pad_w2(w2_k), w1_v.astype(BF16), pad_w2(w2_v), pe_rows(pe_k), pe_rows(pe_v),
      tile2(k_gains[0]), jnp.asarray(cc), jnp.asarray(ca), jnp.asarray(cb))

    ci = np.arange(nc)[None, :]
    sj = np.arange(ns)[:, None]
    ovl = ((ci * CMP_STRIDE < (sj + 1) * SLC_BLOCK) & (ci * CMP_STRIDE + CMP_BLOCK > sj * SLC_BLOCK)
           & (ci < n_cmp)).astype(np.float32)
    n_q = S // tq
    wq = NSA_HEADS * tq
    o_nsa = pl.pallas_call(
        functools.partial(_nsa_kernel, tq_size=tq, nc=nc, n_cmp=n_cmp, ns=ns, top_k=top_k),
        grid=(B, n_q),
        in_specs=[pl.BlockSpec((1, NSA_Q_W, tq), lambda b, i: (b, 0, i)),
                  pl.BlockSpec((1, G, nc, LANES), lambda b, i: (b, 0, 0, 0)),
                  pl.BlockSpec((1, G, D, nc), lambda b, i: (b, 0, 0, 0)),
                  _const_spec((ns, nc)),
                  pl.BlockSpec((1, 2, S, 2 * LANES), lambda b, i: (b, 0, 0, 0)),
                  pl.BlockSpec((1, 2, kvw, S), lambda b, i: (b, 0, 0, 0)),
                  pl.BlockSpec((1, GATE_ROWS, tq), lambda b, i: (b, 0, i))],
        out_specs=pl.BlockSpec((1, tq, NSA_Q_W), lambda b, i: (b, i, 0)),
        out_shape=jax.ShapeDtypeStruct((B, S, NSA_Q_W), BF16),
        scratch_shapes=[pltpu.VMEM((2 * D + LANES, wq), BF16),
                        pltpu.VMEM((tq, wq), BF16),
                        pltpu.VMEM((1, wq), F32),
                        pltpu.VMEM((2, D + 2 * SUBLANES, wq), F32),
                        pltpu.VMEM((2, 1, wq), F32),
                        pltpu.VMEM((D, wq), F32)],
        compiler_params=pltpu.CompilerParams(dimension_semantics=("arbitrary", "arbitrary"),
                                             vmem_limit_bytes=VMEM_LIMIT),
        name="nsa_attn",
    )(qT, kcmp, vcT, jnp.asarray(ovl, BF16), kcat, vcat, gateT)

    g3 = lambda a: a.reshape(B, S, a.shape[-1])
    gspec = lambda w: pl.BlockSpec((1, tg, w), lambda b, t: (b, t, 0))
    hmid = pl.pallas_call(
        functools.partial(_gla_out_kernel, tg=tg),
        grid=(B, S // tg),
        in_specs=[gspec(GLA_QK_W), gspec(GLA_QK_W), gspec(GLA_QK_W), gspec(GLA_V_W), gspec(GLA_V_W),
                  _const_spec((1, GLA_DV)), gspec(DM), gspec(NSA_Q_W), _const_spec((NSA_Q_W + GLA_V_W, DM))],
        out_specs=gspec(DM),
        out_shape=jax.ShapeDtypeStruct((B, S, DM), F32),
        scratch_shapes=[pltpu.VMEM((GLA_HEADS, GLA_DV, GLA_QK_W), F32)],
        compiler_params=pltpu.CompilerParams(dimension_semantics=("arbitrary", "arbitrary"),
                                             vmem_limit_bytes=VMEM_LIMIT),
        name="gla_out",
    )(g3(gq), g3(gk), g3(la), g3(gv), g3(gout), out_gain.reshape(1, GLA_DV), x, o_nsa, w_out.astype(BF16))
    hmid = hmid.reshape(B * S, DM)
    n_t = (B * S) // tm
    row = lambda w: pl.BlockSpec((tm, w), lambda t: (t, 0))

    tf = 256
    assert D_FF % tf == 0
    halo_blocks = tm // SUBLANES
    out = pl.pallas_call(
        functools.partial(_ffn_kernel, tm=tm, tf=tf, tiles_per_seq=n_s),
        grid=(n_t,),
        in_specs=[row(DM),
                  pl.BlockSpec((SUBLANES, DM), lambda t: (jnp.maximum(t * halo_blocks - 1, 0), 0)),
                  _const_spec((1, DM)),
                  pl.BlockSpec((DM, 2 * D_FF), lambda t: (0, 0), pipeline_mode=pl.Buffered(1)),
                  _const_spec((CONV_W, 2 * D_FF)), _const_spec((1, 2 * D_FF)),
                  pl.BlockSpec((D_FF, DM), lambda t: (0, 0), pipeline_mode=pl.Buffered(1))],
        out_specs=row(DM),
        out_shape=jax.ShapeDtypeStruct((B * S, DM), F32),
        scratch_shapes=[pltpu.VMEM((4, tm + SUBLANES, tf), F32), pltpu.VMEM((tm, D_FF), BF16)],
        compiler_params=pltpu.CompilerParams(dimension_semantics=("arbitrary",), vmem_limit_bytes=VMEM_LIMIT),
        name="ffn",
    )(hmid, hmid, ffn_norm.reshape(1, DM), w_up.astype(BF16), conv_w, conv_b.reshape(1, 2 * D_FF),
      w_down.astype(BF16))
    return out.reshape(B, S, DM)
```

```python
import functools

import numpy as np
import jax
import jax.numpy as jnp
from jax import lax
from jax.experimental import pallas as pl
from jax.experimental.pallas import tpu as pltpu

F32 = jnp.float32
BF16 = jnp.bfloat16

NSA_HEADS = 8
NSA_KV_GROUPS = 2
NSA_HEAD_DIM = 64
NSA_HPG = NSA_HEADS // NSA_KV_GROUPS
CMP_BLOCK = 32
CMP_STRIDE = 16
CMP_HIDDEN = 256
SLC_BLOCK = 64
SLC_TOPK = 8
WINDOW = 512
FORCE_SCORE = 1e4
GLA_HEADS = 4
GLA_DK = 64
GLA_DV = 128
GLA_GATE_RANK = 16
GLA_TAU = 16.0
GLA_CHUNK = 64
ROPE_THETA = 500000.0
ROT_DIM = NSA_HEAD_DIM // 4
ROT_HALF = ROT_DIM // 2
D_FF = 2816
CONV_W = 3
EPS = 1e-6
NEG = -1e30
LOG2E = 1.4426950408889634

NSA_Q_W = NSA_HEADS * NSA_HEAD_DIM
NSA_KV_W = NSA_KV_GROUPS * NSA_HEAD_DIM
GLA_QK_W = GLA_HEADS * GLA_DK
GLA_V_W = GLA_HEADS * GLA_DV
IN_SPLITS = [NSA_Q_W, 6 * NSA_KV_W, 3 * NSA_HEADS, GLA_QK_W, GLA_QK_W, GLA_V_W, GLA_GATE_RANK, GLA_V_W]

LANES = 128
SUBLANES = 8
GATE_ROWS = 32
SCORE_AHEAD = 1
VMEM_LIMIT = 56 * 1024 * 1024


def _dot(a, b):
    return jnp.dot(a, b, preferred_element_type=F32)


def _dot_nt(a, b):
    return lax.dot_general(a, b, (((1,), (1,)), ((), ())), preferred_element_type=F32)


def _split_dot(m_bf16, x_f32):
    hi = x_f32.astype(BF16)
    lo = (x_f32 - hi.astype(F32)).astype(BF16)
    return _dot(m_bf16, hi) + _dot(m_bf16, lo)


def _split_dot_r(x_f32, m_bf16):
    hi = x_f32.astype(BF16)
    lo = (x_f32 - hi.astype(F32)).astype(BF16)
    return _dot(hi, m_bf16) + _dot(lo, m_bf16)


def _rope_lanes(y, c, a, b):
    return y * c + pltpu.roll(y, LANES - ROT_HALF, 1) * a + pltpu.roll(y, ROT_HALF, 1) * b


def _sigmoid(z):
    return 1.0 / (1.0 + jnp.exp(-z))


def _in_proj_kernel(x_ref, g_ref, wf_ref, wt_ref, qg_ref, kgs_ref, kgw_ref, bd_ref,
                    cosT_ref, sinT_ref, rc_ref, ra_ref, rb_ref, wg2_ref, bg_ref,
                    qT_ref, vcat_ref, gateT_ref, kcat_ref, kc_ref, vc_ref,
                    gq_ref, gk_ref, la_ref, gv_ref, gout_ref):
    tm = x_ref.shape[0]
    x = x_ref[...]
    ms = jnp.mean(x * x, axis=-1, keepdims=True)
    xn = (x * lax.rsqrt(ms + EPS) * g_ref[...]).astype(BF16)

    offs = np.cumsum([0, NSA_KV_W, NSA_KV_W, 2 * NSA_KV_W, GLA_QK_W, GLA_QK_W, GLA_V_W, GLA_V_W, LANES])
    c_ks, c_kw, c_kvc, c_gq, c_gk, c_gv, c_gout, c_glr, c_end = [int(v) for v in offs]

    def seg(a, b):
        return _dot(xn, wt_ref[:, a:b])

    zf = _dot_nt(wf_ref[...], xn)
    z_ks = seg(c_ks, c_kw)
    z_kw = seg(c_kw, c_kvc)
    cosT = cosT_ref[...]
    sinT = sinT_ref[...]
    qg = qg_ref[...]
    scale = NSA_HEAD_DIM ** -0.5 * LOG2E

    def q_head(h):
        blk = zf[h * NSA_HEAD_DIM:(h + 1) * NSA_HEAD_DIM]
        hms = jnp.mean(blk * blk, axis=0, keepdims=True)
        y = blk * lax.rsqrt(hms + EPS) * qg
        y1 = y[0:ROT_HALF]
        y2 = y[ROT_HALF:ROT_DIM]
        r = jnp.concatenate([y1 * cosT - y2 * sinT, y2 * cosT + y1 * sinT, y[ROT_DIM:]], axis=0)
        qT_ref[0, h * NSA_HEAD_DIM:(h + 1) * NSA_HEAD_DIM, :] = (r * scale).astype(BF16)

    gq_ref[...] = seg(c_gq, c_gk)
    for h in range(0, NSA_HEADS // 2):
        q_head(h)
    gk_ref[...] = seg(c_gk, c_gv)
    for h in range(NSA_HEADS // 2, NSA_HEADS):
        q_head(h)
    gv_ref[...] = seg(c_gv, c_gout).astype(BF16)
    o = NSA_Q_W
    vcat_ref[0, 0] = zf[o:o + NSA_KV_W].astype(BF16)
    vcat_ref[0, 1] = zf[o + NSA_KV_W:o + 2 * NSA_KV_W].astype(BF16)
    gateT_ref[0] = _sigmoid(zf[o + 2 * NSA_KV_W:o + 2 * NSA_KV_W + GATE_ROWS])

    bd = bd_ref[...]
    rc, ra, rb = rc_ref[...], ra_ref[...], rb_ref[...]

    def knorm(z, gain):
        gms = _split_dot_r(z * z, bd)
        return _rope_lanes(z * lax.rsqrt(gms + EPS) * gain, rc, ra, rb).astype(BF16)

    pos = pl.program_id(0) * tm + lax.broadcasted_iota(jnp.int32, (tm, LANES), 0)
    lane = lax.broadcasted_iota(jnp.int32, (tm, LANES), 1)
    onehot = jnp.where(pos // SLC_BLOCK == lane, 1.0, 0.0).astype(BF16)
    zl = jnp.zeros((tm, LANES), BF16)
    gout_ref[...] = seg(c_gout, c_glr).astype(BF16)
    kcat_ref[0, 0] = jnp.concatenate([knorm(z_ks, kgs_ref[...]), onehot], axis=1)
    kvc = seg(c_kvc, c_gq)
    kc_ref[...] = kvc[:, 0:NSA_KV_W]
    vc_ref[...] = kvc[:, NSA_KV_W:2 * NSA_KV_W]
    kcat_ref[0, 1] = jnp.concatenate([knorm(z_kw, kgw_ref[...]), zl], axis=1)
    glr = seg(c_glr, c_end).astype(BF16)
    pre = _dot(glr, wg2_ref[...]) + bg_ref[...]
    la_ref[...] = (jnp.minimum(pre, 0.0) - jnp.log1p(jnp.exp(-jnp.abs(pre)))) * (1.0 / GLA_TAU)


def _gelu_tanh(x):
    return 0.5 * x * (1.0 + jnp.tanh(0.7978845608028654 * (x + 0.044715 * (x * x * x))))


def _compress_kernel(kc_ref, vc_ref, w1k_ref, w2k_ref, w1v_ref, w2v_ref, pek_ref, pev_ref, kg_ref,
                     cc_ref, ca_ref, cb_ref, kcmp_ref, vcT_ref, *, nc):
    half = CMP_STRIDE * NSA_HEAD_DIM
    taps = [[src[0, pl.ds(t, nc, stride=CMP_STRIDE), :] for t in range(CMP_STRIDE)] for src in (kc_ref, vc_ref)]
    for j in range(2 * NSA_KV_GROUPS):
        is_k = j < NSA_KV_GROUPS
        g = j % NSA_KV_GROUPS
        w1 = w1k_ref if is_k else w1v_ref
        w2 = w2k_ref if is_k else w2v_ref
        pe = pek_ref if is_k else pev_ref
        lanes = slice(g * NSA_HEAD_DIM, (g + 1) * NSA_HEAD_DIM)
        r = jnp.concatenate([tp[:, lanes] for tp in taps[0 if is_k else 1]],
                            axis=1).astype(BF16)
        p1 = _dot(r, w1[0:half, :])
        p2 = _dot(r, w1[half:2 * half, :])
        cst = _dot(pe[...], w1[...])[0:1]
        hid = p1 + pltpu.roll(p2, nc - 1, 0) + cst
        out = _dot(_gelu_tanh(hid).astype(BF16), w2[...])
        if is_k:
            kms = jnp.sum(out * out, axis=-1, keepdims=True) * (1.0 / NSA_HEAD_DIM)
            y = out * lax.rsqrt(kms + EPS) * kg_ref[...]
            kcmp_ref[0, g] = _rope_lanes(y, cc_ref[...], ca_ref[...], cb_ref[...]).astype(BF16)
        else:
            vcT_ref[0, g] = out.T[0:NSA_HEAD_DIM].astype(BF16)


def _nsa_kernel(qT_ref, kcmp_ref, vcT_ref, ovl_ref, kcat_ref, vcat_ref, gate_ref, o_ref,
                qall_ref, p_ref, al_ref, acc_ref, m_ref, out_ref,
                *, tq_size, nc, n_cmp, ns, top_k):
    TQ = tq_size
    TK = tq_size
    D = NSA_HEAD_DIM
    G = NSA_KV_GROUPS
    GW = NSA_HPG * TQ
    W = NSA_HEADS * TQ
    WIN, SEL = 0, 1
    P_SEL, P_WIN = 0, 1
    n_back = WINDOW // TK
    i = pl.program_id(1)
    t0 = i * TQ

    zeros = jnp.zeros((D, TQ), BF16)
    for gh in range(NSA_HEADS):
        g = gh // NSA_HPG
        qall_ref[gh, g * D:(g + 1) * D, :] = qT_ref[0, gh * D:(gh + 1) * D, :]
        qall_ref[gh, (1 - g) * D:(2 - g) * D, :] = zeros
        qall_ref[gh, 2 * D:2 * D + LANES, :] = jnp.zeros((LANES, TQ), BF16)
    m_ref[...] = jnp.full(m_ref.shape, NEG, F32)
    acc_ref[...] = jnp.zeros(acc_ref.shape, F32)

    row = lax.broadcasted_iota(jnp.int32, (TK, TQ), 0)
    qcol = lax.broadcasted_iota(jnp.int32, (TK, TQ), 1)
    ones_rows = jnp.ones((2 * SUBLANES, TK), BF16)

    n_full = jnp.minimum(i, n_back - 1)
    last = 2 + n_full + i

    def job(j):
        u = j - 3
        is_full = u < n_full
        plane = jnp.where(is_full, P_WIN, P_SEL)
        kv = jnp.where(is_full, i - 1 - u, u - n_full)
        state = jnp.where(is_full, WIN, SEL)
        plane = jnp.where(j == 0, P_WIN, jnp.where(j == 1, P_SEL, jnp.where(j == 2, P_WIN, plane)))
        kv = jnp.where(j < 2, i, jnp.where(j == 2, jnp.maximum(i - n_back, 0), kv))
        state = jnp.where(j == 0, WIN, jnp.where(j == 1, SEL, jnp.where(j == 2, WIN, state)))
        return state, plane, pl.multiple_of(kv * TK, TK)

    def k_tile(j):
        _, plane, k0 = job(j)
        return kcat_ref[0, plane, pl.ds(k0, TK), :]

    def v_aug(j):
        _, plane, k0 = job(j)
        vT = vcat_ref[0, plane, :, pl.ds(k0, TK)]
        return [jnp.concatenate([vT[g * D:(g + 1) * D], ones_rows], axis=0) for g in range(G)]

    def pv_block(state, vaug, c):
        acc_ref[state, c] = al_ref[c] * acc_ref[state, c] + _dot(vaug[c // NSA_HPG], p_ref[c])

    def sm_block(state, c, s_blk, mask):
        for hb in range(TQ // LANES):
            lanes = slice(hb * LANES, (hb + 1) * LANES)
            s = s_blk[:, lanes]
            if mask is not None:
                s = jnp.where(mask[:, lanes], s, NEG)
            m_old = m_ref[state, c, :, lanes]
            m_new = jnp.maximum(m_old, jnp.max(s, axis=0, keepdims=True))
            al_ref[c, :, lanes] = jnp.exp2(m_old - m_new)
            m_ref[state, c, :, lanes] = m_new
            p_ref[c, :, lanes] = jnp.exp2(s - m_new).astype(BF16)

    def scores(kt, c):
        return _dot(kt, qall_ref[c])

    def step(j, s_first, j_next, prev, mask=None):
        state, _, _ = job(j)
        kt = k_tile(j)
        kt_next = k_tile(j_next)
        if prev is not None:
            state_prev, _, _ = job(prev)
            vaug = v_aug(prev)
        pending = list(s_first)
        for c in range(NSA_HEADS):
            s_cur = pending.pop(0)
            ahead = c + SCORE_AHEAD
            pending.append(scores(kt, ahead) if ahead < NSA_HEADS else scores(kt_next, ahead - NSA_HEADS))
            if prev is not None:
                pv_block(state_prev, vaug, c)
            sm_block(state, c, s_cur, mask)
        return tuple(pending)

    kt0 = k_tile(0)
    s_first = tuple(scores(kt0, c) for c in range(SCORE_AHEAD))

    def gate_row(gh, c):
        return gate_ref[0, gh * 3 + c:gh * 3 + c + 1, :]

    tq_g = t0 + (lax.broadcasted_iota(jnp.int32, (1, GW), 1) & (TQ - 1))
    n_idx = lax.broadcasted_iota(jnp.int32, (nc, GW), 0)
    cmask = (n_idx * CMP_STRIDE + (CMP_BLOCK - 1) <= tq_g) & (n_idx < n_cmp)
    zpad = jnp.zeros((D, GW), BF16)
    groups = range(G)
    gcols = [slice(g * GW, (g + 1) * GW) for g in groups]
    q128 = [jnp.concatenate(
        [jnp.concatenate([qT_ref[0, gh * D:(gh + 1) * D, :] for gh in range(g * NSA_HPG, (g + 1) * NSA_HPG)], axis=1),
         zpad], axis=0) for g in groups]
    s = [jnp.where(cmask, _dot(kcmp_ref[0, g], q128[g]), NEG) for g in groups]
    m = [jnp.max(s[g], axis=0, keepdims=True) for g in groups]
    e = [jnp.where(cmask, jnp.exp2(s[g] - m[g]), 0.0) for g in groups]
    l = [jnp.sum(e[g], axis=0, keepdims=True) for g in groups]
    p = [e[g] * jnp.where(l[g] > 0.0, 1.0 / l[g], 0.0) for g in groups]
    o_cmp = [_dot(vcT_ref[0, g], p[g].astype(BF16)) for g in groups]
    for gh in range(NSA_HEADS):
        g, h = divmod(gh, NSA_HPG)
        out_ref[gh] = gate_row(gh, 0) * o_cmp[g][:, h * TQ:(h + 1) * TQ]
    psum = jnp.concatenate(
        [sum(p[g][:, h * TQ:(h + 1) * TQ] for h in range(1, NSA_HPG)) + p[g][:, 0:TQ] for g in groups], axis=1)
    p_slc = _split_dot(ovl_ref[...], psum)
    tq2 = t0 + (lax.broadcasted_iota(jnp.int32, (1, G * TQ), 1) & (TQ - 1))
    j_idx = lax.broadcasted_iota(jnp.int32, (ns, G * TQ), 0)
    cur = tq2 // SLC_BLOCK
    forced = (j_idx == 0) | (j_idx == cur) | (j_idx == cur - 1)
    score = jnp.where(j_idx <= cur, jnp.where(forced, FORCE_SCORE, p_slc), -jnp.inf)
    ranks = [jnp.zeros((ns, G * TQ), F32), jnp.zeros((ns, G * TQ), F32)]
    for a in range(ns):
        sa = score[a:a + 1, :]
        beats = (sa > score) | ((sa == score) & (j_idx > a))
        ranks[a % 2] = ranks[a % 2] + beats.astype(F32)
    selbias = jnp.where(ranks[0] + ranks[1] < top_k, 0.0, NEG).astype(BF16)
    for gh in range(NSA_HEADS):
        g = gh // NSA_HPG
        qall_ref[gh, 2 * D:2 * D + ns, :] = selbias[:, g * TQ:(g + 1) * TQ]

    diag = row <= qcol
    tail = (row > qcol) & (i >= n_back)
    s_first = step(0, s_first, 1, None, diag)
    s_first = step(1, s_first, 2, 0, diag)
    s_first = step(2, s_first, jnp.minimum(3, last), 1, tail)

    def unmasked(u, s_in):
        j = 3 + u
        return step(j, s_in, jnp.minimum(j + 1, last), j - 1)

    lax.fori_loop(0, last - 2, unmasked, s_first)
    state_last, _, _ = job(last)
    vaug_last = v_aug(last)
    for c in range(NSA_HEADS):
        pv_block(state_last, vaug_last, c)

    def branch_out(state, gh, c):
        return gate_row(gh, c) * (acc_ref[state, gh, 0:D, :] * (1.0 / acc_ref[state, gh, D:D + 1, :]))

    stacked = jnp.concatenate(
        [out_ref[gh] + branch_out(SEL, gh, 1) + branch_out(WIN, gh, 2) for gh in range(NSA_HEADS)], axis=0)
    o_ref[0] = stacked.T.astype(BF16)


def _gla_out_kernel(gq_ref, gk_ref, la_ref, gv_ref, gout_ref, gain_ref, x_ref, on_ref, wo_ref, h_ref,
                    st_ref, *, tg):
    L = GLA_CHUNK

    @pl.when(pl.program_id(1) == 0)
    def _():
        st_ref[...] = jnp.zeros_like(st_ref)

    h_nsa = x_ref[0] + _dot(on_ref[0], wo_ref[0:NSA_Q_W, :])
    la = la_ref[0]
    row = lax.broadcasted_iota(jnp.int32, (tg, tg), 0)
    col = lax.broadcasted_iota(jnp.int32, (tg, tg), 1)
    same = (row // L) == (col // L)
    tril = same & (col <= row)
    n_ch = tg // L
    heads = range(GLA_HEADS)
    chunks = [slice(c * L, (c + 1) * L) for c in range(n_ch)]
    b = _split_dot(tril.astype(BF16), la)
    bl_rows = [b[(c + 1) * L - 1:(c + 1) * L, :] for c in range(n_ch)]
    bl = jnp.concatenate([jnp.broadcast_to(r, (L, GLA_QK_W)) for r in bl_rows], axis=0)
    gk = gk_ref[0]
    qt = (gq_ref[0] * (GLA_DK ** -0.5) * jnp.exp(b))
    kt = (gk * jnp.exp(-b)).astype(BF16)
    kd = (gk * jnp.exp(bl - b)).astype(BF16)
    dec = [jnp.exp(r) for r in bl_rows]
    lane = lax.broadcasted_iota(jnp.int32, (1, GLA_QK_W), 1)
    gain = gain_ref[...]
    qth = [jnp.where((lane // GLA_DK) == h, qt, 0.0).astype(BF16) for h in heads]
    vh = [gv_ref[0, :, h * GLA_DV:(h + 1) * GLA_DV] for h in heads]
    a = [jnp.where(tril, _dot_nt(qth[h], kt), 0.0).astype(BF16) for h in heads]
    vT = [vh[h].astype(F32).T.astype(BF16) for h in heads]
    d_st = [[_dot(vT[h][:, rows], kd[rows]) for rows in chunks] for h in heads]
    o_intra = [_dot(a[h], vh[h]) for h in heads]
    st_in = [[None] * n_ch for _ in heads]
    for h in heads:
        st = st_ref[h]
        for c in range(n_ch):
            st_in[h][c] = st.astype(BF16)
            st = st * dec[c] + d_st[h][c]
        st_ref[h] = st
    outs = []
    for h in heads:
        o = o_intra[h] + jnp.concatenate(
            [_dot_nt(qth[h][rows], st_in[h][c]) for c, rows in enumerate(chunks)], axis=0)
        oms = jnp.mean(o * o, axis=-1, keepdims=True)
        y = o * lax.rsqrt(oms + EPS) * gain
        go = gout_ref[0, :, h * GLA_DV:(h + 1) * GLA_DV].astype(F32)
        outs.append((y * (go * _sigmoid(go))).astype(BF16))
    o_gla = jnp.concatenate(outs, axis=1)
    h_ref[0] = h_nsa + _dot(o_gla, wo_ref[NSA_Q_W:NSA_Q_W + GLA_V_W, :])


def _ffn_kernel(h_ref, halo_ref, g_ref, wup_ref, cw_ref, cb_ref, wdn_ref, o_ref, u_ref, act_ref,
                *, tm, tf, tiles_per_seq):
    h = h_ref[...]
    hh = jnp.concatenate([halo_ref[...], h], axis=0)
    ms = jnp.mean(hh * hh, axis=-1, keepdims=True)
    hn = (hh * lax.rsqrt(ms + EPS) * g_ref[...]).astype(BF16)
    halo_keep = jnp.where((pl.program_id(0) % tiles_per_seq) == 0, 0.0, 1.0)
    lo, hi = SUBLANES, tm + SUBLANES

    def conv(col0, slot):
        u = _dot(hn, wup_ref[:, col0:col0 + tf])
        u_ref[slot, lo:hi, :] = u[lo:hi]
        u_ref[slot, 0:lo, :] = u[0:lo] * halo_keep
        w = cw_ref[:, col0:col0 + tf]
        return (u[lo:hi] * w[2:3] + u_ref[slot, lo - 1:hi - 1, :] * w[1:2]
                + u_ref[slot, lo - 2:hi - 2, :] * w[0:1] + cb_ref[:, col0:col0 + tf])

    for c in range(D_FF // tf):
        gate = conv(c * tf, 2 * (c % 2))
        up = conv(D_FF + c * tf, 2 * (c % 2) + 1)
        act_ref[:, c * tf:(c + 1) * tf] = (gate * _sigmoid(gate) * up).astype(BF16)
    o_ref[...] = h + _dot(act_ref[...], wdn_ref[...])


def _rope_tables(pos):
    inv = ROPE_THETA ** (-np.arange(ROT_HALF, dtype=np.float32) / ROT_HALF)
    ang = pos.astype(np.float32)[:, None] * inv[None, :]
    cos, sin = np.cos(ang), np.sin(ang)
    n = len(pos)
    c = np.ones((n, LANES), np.float32)
    a = np.zeros((n, LANES), np.float32)
    b = np.zeros((n, LANES), np.float32)
    for base in range(0, LANES, NSA_HEAD_DIM):
        c[:, base:base + ROT_HALF] = cos
        c[:, base + ROT_HALF:base + ROT_DIM] = cos
        a[:, base:base + ROT_HALF] = -sin
        b[:, base + ROT_HALF:base + ROT_DIM] = sin
    return cos, sin, c, a, b


def _const_spec(shape):
    nd = len(shape)
    return pl.BlockSpec(shape, lambda *_: (0,) * nd)


def kernel(x, attn_norm, w_in, nsa_q_norm, nsa_k_norm, cmp_pos_k, cmp_pos_v, cmp_w1_k, cmp_w2_k,
           cmp_w1_v, cmp_w2_v, gla_w_gate2, gla_b_gate, gla_out_norm, w_out, ffn_norm, w_up,
           conv_w, conv_b, w_down):
    B, S, DM = x.shape
    depth = attn_norm.shape[0]
    h = x
    for l in range(depth):
        h = _layer(h, attn_norm[l], w_in[l], nsa_q_norm[l], nsa_k_norm[l], cmp_pos_k[l], cmp_pos_v[l],
                   cmp_w1_k[l], cmp_w2_k[l], cmp_w1_v[l], cmp_w2_v[l], gla_w_gate2[l], gla_b_gate[l],
                   gla_out_norm[l], w_out[l], ffn_norm[l], w_up[l], conv_w[l], conv_b[l], w_down[l])
    return h


def _layer(x, attn_norm, w_in, q_gain, k_gains, pe_k, pe_v, w1_k, w2_k, w1_v, w2_v, w_gate2, b_gate,
           out_gain, w_out, ffn_norm, w_up, conv_w, conv_b, w_down):
    B, S, DM = x.shape
    G, D = NSA_KV_GROUPS, NSA_HEAD_DIM
    tm = min(512, S)
    tq = min(256, S)
    tg = min(256, S)
    n_s = S // tm
    nc = S // CMP_STRIDE
    n_cmp = (S - CMP_BLOCK) // CMP_STRIDE + 1
    ns = S // SLC_BLOCK
    top_k = min(SLC_TOPK, ns)
    assert S % tm == 0 and S % tq == 0 and WINDOW % tq == 0 and ns <= LANES and nc % SUBLANES == 0

    offs = np.cumsum([0] + IN_SPLITS)
    nq0, nkv0, ngate0, gq0, gk0, gv0, glr0, gout0 = [int(o) for o in offs[:-1]]
    kvw = NSA_KV_W
    cols = lambda a, n: w_in[:, a:a + n]
    w_feat = jnp.concatenate([cols(nq0, NSA_Q_W), cols(nkv0 + 3 * kvw, kvw), cols(nkv0 + 5 * kvw, kvw),
                              cols(ngate0, 3 * NSA_HEADS),
                              jnp.zeros((DM, GATE_ROWS - 3 * NSA_HEADS), w_in.dtype)], axis=1)
    wf = w_feat.T.astype(BF16)
    w_tok = jnp.concatenate([cols(nkv0 + 2 * kvw, kvw), cols(nkv0 + 4 * kvw, kvw),
                             cols(nkv0, kvw), cols(nkv0 + kvw, kvw),
                             cols(gq0, GLA_QK_W), cols(gk0, GLA_QK_W), cols(gv0, GLA_V_W),
                             cols(gout0, GLA_V_W), cols(glr0, GLA_GATE_RANK),
                             jnp.zeros((DM, LANES - GLA_GATE_RANK), w_in.dtype)], axis=1)
    wt = w_tok.astype(BF16)
    nf, nt = wf.shape[0], wt.shape[1]
    wg2 = jnp.concatenate([w_gate2, jnp.zeros((LANES - GLA_GATE_RANK, GLA_QK_W), w_gate2.dtype)],
                          axis=0).astype(BF16)
    cos, sin, rc, ra, rb = _rope_tables(np.arange(S))
    cosT, sinT = jnp.asarray(cos.T), jnp.asarray(sin.T)
    cmp_end = np.arange(nc) * CMP_STRIDE + CMP_BLOCK - 1
    _, _, cc, ca, cb = _rope_tables(cmp_end)
    bd = np.kron(np.eye(LANES // D, dtype=np.float32), np.full((D, D), 1.0 / D, np.float32))
    tile2 = lambda v: jnp.tile(v.reshape(1, D), (1, LANES // D))

    x2 = x.reshape(B * S, DM)
    tok = lambda w: pl.BlockSpec((tm, w), lambda s, b: (b * n_s + s, 0))
    feat = lambda r: pl.BlockSpec((1, r, tm), lambda s, b: (b, 0, s))
    stab = lambda r, w: pl.BlockSpec((r, w), lambda s, b: (0, s)) if r == ROT_HALF else \
        pl.BlockSpec((r, w), lambda s, b: (s, 0))
    outs = pl.pallas_call(
        _in_proj_kernel,
        grid=(n_s, B),
        in_specs=[tok(DM), _const_spec((1, DM)), _const_spec((nf, DM)), _const_spec((DM, nt)),
                  _const_spec((D, 1)), _const_spec((1, LANES)), _const_spec((1, LANES)),
                  _const_spec((LANES, LANES)),
                  stab(ROT_HALF, tm), stab(ROT_HALF, tm), stab(tm, LANES), stab(tm, LANES), stab(tm, LANES),
                  _const_spec((LANES, GLA_QK_W)), _const_spec((1, GLA_QK_W))],
        out_specs=[feat(NSA_Q_W), pl.BlockSpec((1, 2, kvw, tm), lambda s, b: (b, 0, 0, s)), feat(GATE_ROWS),
                   pl.BlockSpec((1, 2, tm, 2 * LANES), lambda s, b: (b, 0, s, 0)),
                   tok(kvw), tok(kvw), tok(GLA_QK_W), tok(GLA_QK_W), tok(GLA_QK_W),
                   tok(GLA_V_W), tok(GLA_V_W)],
        out_shape=[jax.ShapeDtypeStruct((B, NSA_Q_W, S), BF16), jax.ShapeDtypeStruct((B, 2, kvw, S), BF16),
                   jax.ShapeDtypeStruct((B, GATE_ROWS, S), F32),
                   jax.ShapeDtypeStruct((B, 2, S, 2 * LANES), BF16),
                   jax.ShapeDtypeStruct((B * S, kvw), F32), jax.ShapeDtypeStruct((B * S, kvw), F32),
                   jax.ShapeDtypeStruct((B * S, GLA_QK_W), F32), jax.ShapeDtypeStruct((B * S, GLA_QK_W), F32),
                   jax.ShapeDtypeStruct((B * S, GLA_QK_W), F32),
                   jax.ShapeDtypeStruct((B * S, GLA_V_W), BF16), jax.ShapeDtypeStruct((B * S, GLA_V_W), BF16)],
        compiler_params=pltpu.CompilerParams(dimension_semantics=("arbitrary", "arbitrary"),
                                             vmem_limit_bytes=VMEM_LIMIT),
        name="in_proj",
    )(x2, attn_norm.reshape(1, DM), wf, wt, q_gain.reshape(D, 1), tile2(k_gains[1]), tile2(k_gains[2]),
      jnp.asarray(bd, BF16), cosT, sinT, jnp.asarray(rc), jnp.asarray(ra), jnp.asarray(rb),
      wg2, b_gate.reshape(1, GLA_QK_W))
    qT, vcat, gateT, kcat, kc, vc, gq, gk, la, gv, gout = outs

    seq = lambda a: a.reshape(B, S, kvw)
    pad_w2 = lambda w: jnp.concatenate([w, jnp.zeros((CMP_HIDDEN, LANES - D), w.dtype)], axis=1).astype(BF16)
    pe_rows = lambda p: jnp.tile(p.reshape(1, CMP_BLOCK * D), (SUBLANES, 1)).astype(BF16)
    cin = CMP_BLOCK * D
    kcmp, vcT = pl.pallas_call(
        functools.partial(_compress_kernel, nc=nc),
        grid=(B,),
        in_specs=[pl.BlockSpec((1, S, kvw), lambda b: (b, 0, 0)), pl.BlockSpec((1, S, kvw), lambda b: (b, 0, 0)),
                  _const_spec((cin, CMP_HIDDEN)), _const_spec((CMP_HIDDEN, LANES)),
                  _const_spec((cin, CMP_HIDDEN)), _const_spec((CMP_HIDDEN, LANES)),
                  _const_spec((SUBLANES, cin)), _const_spec((SUBLANES, cin)), _const_spec((1, LANES)),
                  _const_spec((nc, LANES)), _const_spec((nc, LANES)), _const_spec((nc, LANES))],
        out_specs=[pl.BlockSpec((1, G, nc, LANES), lambda b: (b, 0, 0, 0)),
                   pl.BlockSpec((1, G, D, nc), lambda b: (b, 0, 0, 0))],
        out_shape=[jax.ShapeDtypeStruct((B, G, nc, LANES), BF16), jax.ShapeDtypeStruct((B, G, D, nc), BF16)],
        compiler_params=pltpu.CompilerParams(dimension_semantics=("arbitrary",), vmem_limit_bytes=VMEM_LIMIT),
        name="compress",
    )(seq(kc), seq(vc), w1_k.astype(BF16), pad_w2(w2_k), w1_v.astype(BF16), pad_w2(w2_v), pe_rows(pe_k), pe_rows(pe_v),
      tile2(k_gains[0]), jnp.asarray(cc), jnp.asarray(ca), jnp.asarray(cb))

    ci = np.arange(nc)[None, :]
    sj = np.arange(ns)[:, None]
    ovl = ((ci * CMP_STRIDE < (sj + 1) * SLC_BLOCK) & (ci * CMP_STRIDE + CMP_BLOCK > sj * SLC_BLOCK)
           & (ci < n_cmp)).astype(np.float32)
    n_q = S // tq
    wq = NSA_HEADS * tq
    o_nsa = pl.pallas_call(
        functools.partial(_nsa_kernel, tq_size=tq, nc=nc, n_cmp=n_cmp, ns=ns, top_k=top_k),
        grid=(B, n_q),
        in_specs=[pl.BlockSpec((1, NSA_Q_W, tq), lambda b, i: (b, 0, i)),
                  pl.BlockSpec((1, G, nc, LANES), lambda b, i: (b, 0, 0, 0)),
                  pl.BlockSpec((1, G, D, nc), lambda b, i: (b, 0, 0, 0)),
                  _const_spec((ns, nc)),
                  pl.BlockSpec((1, 2, S, 2 * LANES), lambda b, i: (b, 0, 0, 0)),
                  pl.BlockSpec((1, 2, kvw, S), lambda b, i: (b, 0, 0, 0)),
                  pl.BlockSpec((1, GATE_ROWS, tq), lambda b, i: (b, 0, i))],
        out_specs=pl.BlockSpec((1, tq, NSA_Q_W), lambda b, i: (b, i, 0)),
        out_shape=jax.ShapeDtypeStruct((B, S, NSA_Q_W), BF16),
        scratch_shapes=[pltpu.VMEM((NSA_HEADS, 2 * D + LANES, tq), BF16),
                        pltpu.VMEM((NSA_HEADS, tq, tq), BF16),
                        pltpu.VMEM((NSA_HEADS, 1, tq), F32),
                        pltpu.VMEM((2, NSA_HEADS, D + 2 * SUBLANES, tq), F32),
                        pltpu.VMEM((2, NSA_HEADS, 1, tq), F32),
                        pltpu.VMEM((NSA_HEADS, D, tq), F32)],
        compiler_params=pltpu.CompilerParams(dimension_semantics=("arbitrary", "arbitrary"),
                                             vmem_limit_bytes=VMEM_LIMIT),
        name="nsa_attn",
    )(qT, kcmp, vcT, jnp.asarray(ovl, BF16), kcat, vcat, gateT)

    g3 = lambda a: a.reshape(B, S, a.shape[-1])
    gspec = lambda w: pl.BlockSpec((1, tg, w), lambda b, t: (b, t, 0))
    hmid = pl.pallas_call(
        functools.partial(_gla_out_kernel, tg=tg),
        grid=(B, S // tg),
        in_specs=[gspec(GLA_QK_W), gspec(GLA_QK_W), gspec(GLA_QK_W), gspec(GLA_V_W), gspec(GLA_V_W),
                  _const_spec((1, GLA_DV)), gspec(DM), gspec(NSA_Q_W), _const_spec((NSA_Q_W + GLA_V_W, DM))],
        out_specs=gspec(DM),
        out_shape=jax.ShapeDtypeStruct((B, S, DM), F32),
        scratch_shapes=[pltpu.VMEM((GLA_HEADS, GLA_DV, GLA_QK_W), F32)],
        compiler_params=pltpu.CompilerParams(dimension_semantics=("arbitrary", "arbitrary"),
                                             vmem_limit_bytes=VMEM_LIMIT),
        name="gla_out",
    )(g3(gq), g3(gk), g3(la), g3(gv), g3(gout), out_gain.reshape(1, GLA_DV), x, o_nsa, w_out.astype(BF16))
    hmid = hmid.reshape(B * S, DM)
    n_t = (B * S) // tm
    row = lambda w: pl.BlockSpec((tm, w), lambda t: (t, 0))

    tf = 256
    assert D_FF % tf == 0
    halo_blocks = tm // SUBLANES
    out = pl.pallas_call(
        functools.partial(_ffn_kernel, tm=tm, tf=tf, tiles_per_seq=n_s),
        grid=(n_t,),
        in_specs=[row(DM),
                  pl.BlockSpec((SUBLANES, DM), lambda t: (jnp.maximum(t * halo_blocks - 1, 0), 0)),
                  _const_spec((1, DM)),
                  pl.BlockSpec((DM, 2 * D_FF), lambda t: (0, 0), pipeline_mode=pl.Buffered(1)),
                  _const_spec((CONV_W, 2 * D_FF)), _const_spec((1, 2 * D_FF)),
                  pl.BlockSpec((D_FF, DM), lambda t: (0, 0), pipeline_mode=pl.Buffered(1))],
        out_specs=row(DM),
        out_shape=jax.ShapeDtypeStruct((B * S, DM), F32),
        scratch_shapes=[pltpu.VMEM((4, tm + SUBLANES, tf), F32), pltpu.VMEM((tm, D_FF), BF16)],
        compiler_params=pltpu.CompilerParams(dimension_semantics=("arbitrary",), vmem_limit_bytes=VMEM_LIMIT),
        name="ffn",
    )(hmid, hmid, ffn_norm.reshape(1, DM), w_up.astype(BF16), conv_w, conv_b.reshape(1, 2 * D_FF),
      w_down.astype(BF16))
    return out.reshape(B, S, DM)
```

```python
import functools

import numpy as np
import jax
import jax.numpy as jnp
from jax import lax
from jax.experimental import pallas as pl
from jax.experimental.pallas import tpu as pltpu

F32 = jnp.float32
BF16 = jnp.bfloat16

NSA_HEADS = 8
NSA_KV_GROUPS = 2
NSA_HEAD_DIM = 64
NSA_HPG = NSA_HEADS // NSA_KV_GROUPS
CMP_BLOCK = 32
CMP_STRIDE = 16
CMP_HIDDEN = 256
SLC_BLOCK = 64
SLC_TOPK = 8
WINDOW = 512
FORCE_SCORE = 1e4
GLA_HEADS = 4
GLA_DK = 64
GLA_DV = 128
GLA_GATE_RANK = 16
GLA_TAU = 16.0
GLA_CHUNK = 64
ROPE_THETA = 500000.0
ROT_DIM = NSA_HEAD_DIM // 4
ROT_HALF = ROT_DIM // 2
D_FF = 2816
CONV_W = 3
EPS = 1e-6
NEG = -1e30
LOG2E = 1.4426950408889634

NSA_Q_W = NSA_HEADS * NSA_HEAD_DIM
NSA_KV_W = NSA_KV_GROUPS * NSA_HEAD_DIM
GLA_QK_W = GLA_HEADS * GLA_DK
GLA_V_W = GLA_HEADS * GLA_DV
IN_SPLITS = [NSA_Q_W, 6 * NSA_KV_W, 3 * NSA_HEADS, GLA_QK_W, GLA_QK_W, GLA_V_W, GLA_GATE_RANK, GLA_V_W]

LANES = 128
SUBLANES = 8
GATE_ROWS = 32
SCORE_AHEAD = 1
VMEM_LIMIT = 56 * 1024 * 1024


def _dot(a, b):
    return jnp.dot(a, b, preferred_element_type=F32)


def _dot_nt(a, b):
    return lax.dot_general(a, b, (((1,), (1,)), ((), ())), preferred_element_type=F32)


def _split_dot(m_bf16, x_f32):
    hi = x_f32.astype(BF16)
    lo = (x_f32 - hi.astype(F32)).astype(BF16)
    return _dot(m_bf16, hi) + _dot(m_bf16, lo)


def _split_dot_r(x_f32, m_bf16):
    hi = x_f32.astype(BF16)
    lo = (x_f32 - hi.astype(F32)).astype(BF16)
    return _dot(hi, m_bf16) + _dot(lo, m_bf16)


def _rope_lanes(y, c, a, b):
    return y * c + pltpu.roll(y, LANES - ROT_HALF, 1) * a + pltpu.roll(y, ROT_HALF, 1) * b


def _sigmoid(z):
    return 1.0 / (1.0 + jnp.exp(-z))


def _in_proj_kernel(x_ref, g_ref, wf_ref, wt_ref, qg_ref, kgs_ref, kgw_ref, bd_ref,
                    cosT_ref, sinT_ref, rc_ref, ra_ref, rb_ref, wg2_ref, bg_ref,
                    qT_ref, vcat_ref, gateT_ref, kcat_ref, kc_ref, vc_ref,
                    gq_ref, gk_ref, la_ref, gv_ref, gout_ref):
    tm = x_ref.shape[0]
    x = x_ref[...]
    ms = jnp.mean(x * x, axis=-1, keepdims=True)
    xn = (x * lax.rsqrt(ms + EPS) * g_ref[...]).astype(BF16)

    offs = np.cumsum([0, NSA_KV_W, NSA_KV_W, 2 * NSA_KV_W, GLA_QK_W, GLA_QK_W, GLA_V_W, GLA_V_W, LANES])
    c_ks, c_kw, c_kvc, c_gq, c_gk, c_gv, c_gout, c_glr, c_end = [int(v) for v in offs]

    def seg(a, b):
        return _dot(xn, wt_ref[:, a:b])

    zf = _dot_nt(wf_ref[...], xn)
    z_ks = seg(c_ks, c_kw)
    z_kw = seg(c_kw, c_kvc)
    cosT = cosT_ref[...]
    sinT = sinT_ref[...]
    qg = qg_ref[...]
    scale = NSA_HEAD_DIM ** -0.5 * LOG2E

    def q_head(h):
        blk = zf[h * NSA_HEAD_DIM:(h + 1) * NSA_HEAD_DIM]
        hms = jnp.mean(blk * blk, axis=0, keepdims=True)
        y = blk * lax.rsqrt(hms + EPS) * qg
        y1 = y[0:ROT_HALF]
        y2 = y[ROT_HALF:ROT_DIM]
        r = jnp.concatenate([y1 * cosT - y2 * sinT, y2 * cosT + y1 * sinT, y[ROT_DIM:]], axis=0)
        qT_ref[0, h * NSA_HEAD_DIM:(h + 1) * NSA_HEAD_DIM, :] = (r * scale).astype(BF16)

    gq_ref[...] = seg(c_gq, c_gk)
    for h in range(0, NSA_HEADS // 2):
        q_head(h)
    gk_ref[...] = seg(c_gk, c_gv)
    for h in range(NSA_HEADS // 2, NSA_HEADS):
        q_head(h)
    gv_ref[...] = seg(c_gv, c_gout).astype(BF16)
    o = NSA_Q_W
    vcat_ref[0, 0] = zf[o:o + NSA_KV_W].astype(BF16)
    vcat_ref[0, 1] = zf[o + NSA_KV_W:o + 2 * NSA_KV_W].astype(BF16)
    gateT_ref[0] = _sigmoid(zf[o + 2 * NSA_KV_W:o + 2 * NSA_KV_W + GATE_ROWS])

    bd = bd_ref[...]
    rc, ra, rb = rc_ref[...], ra_ref[...], rb_ref[...]

    def knorm(z, gain):
        gms = _split_dot_r(z * z, bd)
        return _rope_lanes(z * lax.rsqrt(gms + EPS) * gain, rc, ra, rb).astype(BF16)

    pos = pl.program_id(0) * tm + lax.broadcasted_iota(jnp.int32, (tm, LANES), 0)
    lane = lax.broadcasted_iota(jnp.int32, (tm, LANES), 1)
    onehot = jnp.where(pos // SLC_BLOCK == lane, 1.0, 0.0).astype(BF16)
    zl = jnp.zeros((tm, LANES), BF16)
    gout_ref[...] = seg(c_gout, c_glr).astype(BF16)
    kcat_ref[0, 0] = jnp.concatenate([knorm(z_ks, kgs_ref[...]), onehot], axis=1)
    kvc = seg(c_kvc, c_gq)
    kc_ref[...] = kvc[:, 0:NSA_KV_W]
    vc_ref[...] = kvc[:, NSA_KV_W:2 * NSA_KV_W]
    kcat_ref[0, 1] = jnp.concatenate([knorm(z_kw, kgw_ref[...]), zl], axis=1)
    glr = seg(c_glr, c_end).astype(BF16)
    pre = _dot(glr, wg2_ref[...]) + bg_ref[...]
    la_ref[...] = (jnp.minimum(pre, 0.0) - jnp.log1p(jnp.exp(-jnp.abs(pre)))) * (1.0 / GLA_TAU)


def _gelu_tanh(x):
    return 0.5 * x * (1.0 + jnp.tanh(0.7978845608028654 * (x + 0.044715 * (x * x * x))))


def _compress_kernel(kc_ref, vc_ref, w1k_ref, w2k_ref, w1v_ref, w2v_ref, pek_ref, pev_ref, kg_ref,
                     cc_ref, ca_ref, cb_ref, kcmp_ref, vcT_ref, *, nc):
    half = CMP_STRIDE * NSA_HEAD_DIM
    taps = [[src[0, pl.ds(t, nc, stride=CMP_STRIDE), :] for t in range(CMP_STRIDE)] for src in (kc_ref, vc_ref)]
    for j in range(2 * NSA_KV_GROUPS):
        is_k = j < NSA_KV_GROUPS
        g = j % NSA_KV_GROUPS
        w1 = w1k_ref if is_k else w1v_ref
        w2 = w2k_ref if is_k else w2v_ref
        pe = pek_ref if is_k else pev_ref
        lanes = slice(g * NSA_HEAD_DIM, (g + 1) * NSA_HEAD_DIM)
        r = jnp.concatenate([tp[:, lanes] for tp in taps[0 if is_k else 1]],
                            axis=1).astype(BF16)
        p1 = _dot(r, w1[0:half, :])
        p2 = _dot(r, w1[half:2 * half, :])
        cst = _dot(pe[...], w1[...])[0:1]
        hid = p1 + pltpu.roll(p2, nc - 1, 0) + cst
        out = _dot(_gelu_tanh(hid).astype(BF16), w2[...])
        if is_k:
            kms = jnp.sum(out * out, axis=-1, keepdims=True) * (1.0 / NSA_HEAD_DIM)
            y = out * lax.rsqrt(kms + EPS) * kg_ref[...]
            kcmp_ref[0, g] = _rope_lanes(y, cc_ref[...], ca_ref[...], cb_ref[...]).astype(BF16)
        else:
            vcT_ref[0, g] = out.T[0:NSA_HEAD_DIM].astype(BF16)


def _nsa_kernel(qT_ref, kcmp_ref, vcT_ref, ovl_ref, kcat_ref, vcat_ref, gate_ref, o_ref,
                qall_ref, p_ref, al_ref, acc_ref, m_ref, out_ref,
                *, tq_size, nc, n_cmp, ns, top_k):
    TQ = tq_size
    TK = tq_size
    D = NSA_HEAD_DIM
    G = NSA_KV_GROUPS
    GW = NSA_HPG * TQ
    W = NSA_HEADS * TQ
    WIN, SEL = 0, 1
    P_SEL, P_WIN = 0, 1
    n_back = WINDOW // TK
    i = pl.program_id(1)
    t0 = i * TQ

    zeros = jnp.zeros((D, TQ), BF16)
    for gh in range(NSA_HEADS):
        g = gh // NSA_HPG
        qall_ref[gh, g * D:(g + 1) * D, :] = qT_ref[0, gh * D:(gh + 1) * D, :]
        qall_ref[gh, (1 - g) * D:(2 - g) * D, :] = zeros
        qall_ref[gh, 2 * D:2 * D + LANES, :] = jnp.zeros((LANES, TQ), BF16)
    m_ref[...] = jnp.full(m_ref.shape, NEG, F32)
    acc_ref[...] = jnp.zeros(acc_ref.shape, F32)

    row = lax.broadcasted_iota(jnp.int32, (TK, TQ), 0)
    qcol = lax.broadcasted_iota(jnp.int32, (TK, TQ), 1)
    ones_rows = jnp.ones((2 * SUBLANES, TK), BF16)

    n_full = jnp.minimum(i, n_back - 1)
    last = 2 + n_full + i

    def job(j):
        u = j - 3
        is_full = u < n_full
        plane = jnp.where(is_full, P_WIN, P_SEL)
        kv = jnp.where(is_full, i - 1 - u, u - n_full)
        state = jnp.where(is_full, WIN, SEL)
        plane = jnp.where(j == 0, P_WIN, jnp.where(j == 1, P_SEL, jnp.where(j == 2, P_WIN, plane)))
        kv = jnp.where(j < 2, i, jnp.where(j == 2, jnp.maximum(i - n_back, 0), kv))
        state = jnp.where(j == 0, WIN, jnp.where(j == 1, SEL, jnp.where(j == 2, WIN, state)))
        return state, plane, pl.multiple_of(kv * TK, TK)

    def k_tile(j):
        _, plane, k0 = job(j)
        return kcat_ref[0, plane, pl.ds(k0, TK), :]

    def v_aug(j):
        _, plane, k0 = job(j)
        vT = vcat_ref[0, plane, :, pl.ds(k0, TK)]
        return [jnp.concatenate([vT[g * D:(g + 1) * D], ones_rows], axis=0) for g in range(G)]

    def pv_block(state, vaug, c):
        acc_ref[state, c] = al_ref[c] * acc_ref[state, c] + _dot(vaug[c // NSA_HPG], p_ref[c])

    def sm_block(state, c, s_blk, mask):
        for hb in range(TQ // LANES):
            lanes = slice(hb * LANES, (hb + 1) * LANES)
            s = s_blk[:, lanes]
            if mask is not None:
                s = jnp.where(mask[:, lanes], s, NEG)
            m_old = m_ref[state, c, :, lanes]
            m_new = jnp.maximum(m_old, jnp.max(s, axis=0, keepdims=True))
            al_ref[c, :, lanes] = jnp.exp2(m_old - m_new)
            m_ref[state, c, :, lanes] = m_new
            p_ref[c, :, lanes] = jnp.exp2(s - m_new).astype(BF16)

    def scores(kt, c):
        return _dot(kt, qall_ref[c])

    def step(j, s_first, j_next, prev, mask=None):
        state, _, _ = job(j)
        kt = k_tile(j)
        kt_next = k_tile(j_next)
        if prev is not None:
            state_prev, _, _ = job(prev)
            vaug = v_aug(prev)
        pending = list(s_first)
        for c in range(NSA_HEADS):
            s_cur = pending.pop(0)
            ahead = c + SCORE_AHEAD
            pending.append(scores(kt, ahead) if ahead < NSA_HEADS else scores(kt_next, ahead - NSA_HEADS))
            if prev is not None:
                pv_block(state_prev, vaug, c)
            sm_block(state, c, s_cur, mask)
        return tuple(pending)

    kt0 = k_tile(0)
    s_first = tuple(scores(kt0, c) for c in range(SCORE_AHEAD))

    def gate_row(gh, c):
        return gate_ref[0, gh * 3 + c:gh * 3 + c + 1, :]

    tq_g = t0 + (lax.broadcasted_iota(jnp.int32, (1, GW), 1) & (TQ - 1))
    n_idx = lax.broadcasted_iota(jnp.int32, (nc, GW), 0)
    cmask = (n_idx * CMP_STRIDE + (CMP_BLOCK - 1) <= tq_g) & (n_idx < n_cmp)
    zpad = jnp.zeros((D, GW), BF16)
    groups = range(G)
    gcols = [slice(g * GW, (g + 1) * GW) for g in groups]
    q128 = [jnp.concatenate(
        [jnp.concatenate([qT_ref[0, gh * D:(gh + 1) * D, :] for gh in range(g * NSA_HPG, (g + 1) * NSA_HPG)], axis=1),
         zpad], axis=0) for g in groups]
    s = [jnp.where(cmask, _dot(kcmp_ref[0, g], q128[g]), NEG) for g in groups]
    m = [jnp.max(s[g], axis=0, keepdims=True) for g in groups]
    e = [jnp.where(cmask, jnp.exp2(s[g] - m[g]), 0.0) for g in groups]
    l = [jnp.sum(e[g], axis=0, keepdims=True) for g in groups]
    p = [e[g] * jnp.where(l[g] > 0.0, 1.0 / l[g], 0.0) for g in groups]
    o_cmp = [_dot(vcT_ref[0, g], p[g].astype(BF16)) for g in groups]
    for gh in range(NSA_HEADS):
        g, h = divmod(gh, NSA_HPG)
        out_ref[gh] = gate_row(gh, 0) * o_cmp[g][:, h * TQ:(h + 1) * TQ]
    psum = jnp.concatenate(
        [sum(p[g][:, h * TQ:(h + 1) * TQ] for h in range(1, NSA_HPG)) + p[g][:, 0:TQ] for g in groups], axis=1)
    p_slc = _split_dot(ovl_ref[...], psum)
    tq2 = t0 + (lax.broadcasted_iota(jnp.int32, (1, G * TQ), 1) & (TQ - 1))
    j_idx = lax.broadcasted_iota(jnp.int32, (ns, G * TQ), 0)
    cur = tq2 // SLC_BLOCK
    forced = (j_idx == 0) | (j_idx == cur) | (j_idx == cur - 1)
    score = jnp.where(j_idx <= cur, jnp.where(forced, FORCE_SCORE, p_slc), -jnp.inf)
    ranks = [jnp.zeros((ns, G * TQ), F32), jnp.zeros((ns, G * TQ), F32)]
    for a in range(ns):
        sa = score[a:a + 1, :]
        beats = (sa > score) | ((sa == score) & (j_idx > a))
        ranks[a % 2] = ranks[a % 2] + beats.astype(F32)
    selbias = jnp.where(ranks[0] + ranks[1] < top_k, 0.0, NEG).astype(BF16)
    for gh in range(NSA_HEADS):
        g = gh // NSA_HPG
        qall_ref[gh, 2 * D:2 * D + ns, :] = selbias[:, g * TQ:(g + 1) * TQ]

    diag = row <= qcol
    tail = (row > qcol) & (i >= n_back)
    s_first = step(0, s_first, 1, None, diag)
    s_first = step(1, s_first, 2, 0, diag)
    s_first = step(2, s_first, jnp.minimum(3, last), 1, tail)

    n_unmasked = last - 2

    def unmasked_pair(v, s_in):
        j = 3 + 2 * v
        s_mid = step(j, s_in, j + 1, j - 1)
        return step(j + 1, s_mid, jnp.minimum(j + 2, last), j)

    def unmasked_last(_, s_in):
        return step(last, s_in, last, last - 1)

    s_first = lax.fori_loop(0, n_unmasked // 2, unmasked_pair, s_first)
    lax.fori_loop(0, n_unmasked % 2, unmasked_last, s_first)
    state_last, _, _ = job(last)
    vaug_last = v_aug(last)
    for c in range(NSA_HEADS):
        pv_block(state_last, vaug_last, c)

    def branch_out(state, gh, c):
        return gate_row(gh, c) * (acc_ref[state, gh, 0:D, :] * (1.0 / acc_ref[state, gh, D:D + 1, :]))

    stacked = jnp.concatenate(
        [out_ref[gh] + branch_out(SEL, gh, 1) + branch_out(WIN, gh, 2) for gh in range(NSA_HEADS)], axis=0)
    o_ref[0] = stacked.T.astype(BF16)


def _gla_out_kernel(gq_ref, gk_ref, la_ref, gv_ref, gout_ref, gain_ref, x_ref, on_ref, wo_ref, h_ref,
                    st_ref, *, tg):
    L = GLA_CHUNK

    @pl.when(pl.program_id(1) == 0)
    def _():
        st_ref[...] = jnp.zeros_like(st_ref)

    h_nsa = x_ref[0] + _dot(on_ref[0], wo_ref[0:NSA_Q_W, :])
    la = la_ref[0]
    row = lax.broadcasted_iota(jnp.int32, (tg, tg), 0)
    col = lax.broadcasted_iota(jnp.int32, (tg, tg), 1)
    same = (row // L) == (col // L)
    tril = same & (col <= row)
    n_ch = tg // L
    heads = range(GLA_HEADS)
    chunks = [slice(c * L, (c + 1) * L) for c in range(n_ch)]
    vh = [gv_ref[0, :, h * GLA_DV:(h + 1) * GLA_DV] for h in heads]
    vT = [vh[h].astype(F32).T.astype(BF16) for h in heads]
    b = _split_dot(tril.astype(BF16), la)
    bl_rows = [b[(c + 1) * L - 1:(c + 1) * L, :] for c in range(n_ch)]
    bl = jnp.concatenate([jnp.broadcast_to(r, (L, GLA_QK_W)) for r in bl_rows], axis=0)
    gk = gk_ref[0]
    qt = (gq_ref[0] * (GLA_DK ** -0.5) * jnp.exp(b))
    kt = (gk * jnp.exp(-b)).astype(BF16)
    kd = (gk * jnp.exp(bl - b)).astype(BF16)
    dec = [jnp.exp(r) for r in bl_rows]
    lane = lax.broadcasted_iota(jnp.int32, (1, GLA_QK_W), 1)
    gain = gain_ref[...]
    qth = [jnp.where((lane // GLA_DK) == h, qt, 0.0).astype(BF16) for h in heads]
    a = [jnp.where(tril, _dot_nt(qth[h], kt), 0.0).astype(BF16) for h in heads]
    d_st =[[_dot(vT[h][:, rows], kd[rows]) for rows in chunks] for h in heads]
    o_intra = [_dot(a[h], vh[h]) for h in heads]
    st_in = [[None] * n_ch for _ in heads]
    for h in heads:
        st = st_ref[h]
        for c in range(n_ch):
            st_in[h][c] = st.astype(BF16)
            st = st * dec[c] + d_st[h][c]
        st_ref[h] = st
    outs = []
    for h in heads:
        o = o_intra[h] + jnp.concatenate(
            [_dot_nt(qth[h][rows], st_in[h][c]) for c, rows in enumerate(chunks)], axis=0)
        oms = jnp.mean(o * o, axis=-1, keepdims=True)
        y = o * lax.rsqrt(oms + EPS) * gain
        go = gout_ref[0, :, h * GLA_DV:(h + 1) * GLA_DV].astype(F32)
        outs.append((y * (go * _sigmoid(go))).astype(BF16))
    o_gla = jnp.concatenate(outs, axis=1)
    h_ref[0] = h_nsa + _dot(o_gla, wo_ref[NSA_Q_W:NSA_Q_W + GLA_V_W, :])


def _ffn_kernel(h_ref, halo_ref, g_ref, wup_ref, cw_ref, cb_ref, wdn_ref, o_ref, u_ref, act_ref,
                *, tm, tf, tiles_per_seq):
    h = h_ref[...]
    hh = jnp.concatenate([halo_ref[...], h], axis=0)
    ms = jnp.mean(hh * hh, axis=-1, keepdims=True)
    hn = (hh * lax.rsqrt(ms + EPS) * g_ref[...]).astype(BF16)
    halo_keep = jnp.where((pl.program_id(0) % tiles_per_seq) == 0, 0.0, 1.0)
    lo, hi = SUBLANES, tm + SUBLANES

    def conv(col0, slot):
        u = _dot(hn, wup_ref[:, col0:col0 + tf])
        u_ref[slot, lo:hi, :] = u[lo:hi]
        u_ref[slot, 0:lo, :] = u[0:lo] * halo_keep
        w = cw_ref[:, col0:col0 + tf]
        return (u[lo:hi] * w[2:3] + u_ref[slot, lo - 1:hi - 1, :] * w[1:2]
                + u_ref[slot, lo - 2:hi - 2, :] * w[0:1] + cb_ref[:, col0:col0 + tf])

    for c in range(D_FF // tf):
        gate = conv(c * tf, 2 * (c % 2))
        up = conv(D_FF + c * tf, 2 * (c % 2) + 1)
        act_ref[:, c * tf:(c + 1) * tf] = (gate * _sigmoid(gate) * up).astype(BF16)
    o_ref[...] = h + _dot(act_ref[...], wdn_ref[...])


def _rope_tables(pos):
    inv = ROPE_THETA ** (-np.arange(ROT_HALF, dtype=np.float32) / ROT_HALF)
    ang = pos.astype(np.float32)[:, None] * inv[None, :]
    cos, sin = np.cos(ang), np.sin(ang)
    n = len(pos)
    c = np.ones((n, LANES), np.float32)
    a = np.zeros((n, LANES), np.float32)
    b = np.zeros((n, LANES), np.float32)
    for base in range(0, LANES, NSA_HEAD_DIM):
        c[:, base:base + ROT_HALF] = cos
        c[:, base + ROT_HALF:base + ROT_DIM] = cos
        a[:, base:base + ROT_HALF] = -sin
        b[:, base + ROT_HALF:base + ROT_DIM] = sin
    return cos, sin, c, a, b


def _const_spec(shape):
    nd = len(shape)
    return pl.BlockSpec(shape, lambda *_: (0,) * nd)


def kernel(x, attn_norm, w_in, nsa_q_norm, nsa_k_norm, cmp_pos_k, cmp_pos_v, cmp_w1_k, cmp_w2_k,
           cmp_w1_v, cmp_w2_v, gla_w_gate2, gla_b_gate, gla_out_norm, w_out, ffn_norm, w_up,
           conv_w, conv_b, w_down):
    B, S, DM = x.shape
    depth = attn_norm.shape[0]
    h = x
    for l in range(depth):
        h = _layer(h, attn_norm[l], w_in[l], nsa_q_norm[l], nsa_k_norm[l], cmp_pos_k[l], cmp_pos_v[l],
                   cmp_w1_k[l], cmp_w2_k[l], cmp_w1_v[l], cmp_w2_v[l], gla_w_gate2[l], gla_b_gate[l],
                   gla_out_norm[l], w_out[l], ffn_norm[l], w_up[l], conv_w[l], conv_b[l], w_down[l])
    return h


def _layer(x, attn_norm, w_in, q_gain, k_gains, pe_k, pe_v, w1_k, w2_k, w1_v, w2_v, w_gate2, b_gate,
           out_gain, w_out, ffn_norm, w_up, conv_w, conv_b, w_down):
    B, S, DM = x.shape
    G, D = NSA_KV_GROUPS, NSA_HEAD_DIM
    tm = min(512, S)
    tq = min(256, S)
    tg = min(256, S)
    n_s = S // tm
    nc = S // CMP_STRIDE
    n_cmp = (S - CMP_BLOCK) // CMP_STRIDE + 1
    ns = S // SLC_BLOCK
    top_k = min(SLC_TOPK, ns)
    assert S % tm == 0 and S % tq == 0 and WINDOW % tq == 0 and ns <= LANES and nc % SUBLANES == 0

    offs = np.cumsum([0] + IN_SPLITS)
    nq0, nkv0, ngate0, gq0, gk0, gv0, glr0, gout0 = [int(o) for o in offs[:-1]]
    kvw = NSA_KV_W
    cols = lambda a, n: w_in[:, a:a + n]
    w_feat = jnp.concatenate([cols(nq0, NSA_Q_W), cols(nkv0 + 3 * kvw, kvw), cols(nkv0 + 5 * kvw, kvw),
                              cols(ngate0, 3 * NSA_HEADS),
                              jnp.zeros((DM, GATE_ROWS - 3 * NSA_HEADS), w_in.dtype)], axis=1)
    wf = w_feat.T.astype(BF16)
    w_tok = jnp.concatenate([cols(nkv0 + 2 * kvw, kvw), cols(nkv0 + 4 * kvw, kvw),
                             cols(nkv0, kvw), cols(nkv0 + kvw, kvw),
                             cols(gq0, GLA_QK_W), cols(gk0, GLA_QK_W), cols(gv0, GLA_V_W),
                             cols(gout0, GLA_V_W), cols(glr0, GLA_GATE_RANK),
                             jnp.zeros((DM, LANES - GLA_GATE_RANK), w_in.dtype)], axis=1)
    wt = w_tok.astype(BF16)
    nf, nt = wf.shape[0], wt.shape[1]
    wg2 = jnp.concatenate([w_gate2, jnp.zeros((LANES - GLA_GATE_RANK, GLA_QK_W), w_gate2.dtype)],
                          axis=0).astype(BF16)
    cos, sin, rc, ra, rb = _rope_tables(np.arange(S))
    cosT, sinT = jnp.asarray(cos.T), jnp.asarray(sin.T)
    cmp_end = np.arange(nc) * CMP_STRIDE + CMP_BLOCK - 1
    _, _, cc, ca, cb = _rope_tables(cmp_end)
    bd = np.kron(np.eye(LANES // D, dtype=np.float32), np.full((D, D), 1.0 / D, np.float32))
    tile2 = lambda v: jnp.tile(v.reshape(1, D), (1, LANES // D))

    x2 = x.reshape(B * S, DM)
    tok = lambda w: pl.BlockSpec((tm, w), lambda s, b: (b * n_s + s, 0))
    feat = lambda r: pl.BlockSpec((1, r, tm), lambda s, b: (b, 0, s))
    stab = lambda r, w: pl.BlockSpec((r, w), lambda s, b: (0, s)) if r == ROT_HALF else \
        pl.BlockSpec((r, w), lambda s, b: (s, 0))
    outs = pl.pallas_call(
        _in_proj_kernel,
        grid=(n_s, B),
        in_specs=[tok(DM), _const_spec((1, DM)), _const_spec((nf, DM)), _const_spec((DM, nt)),
                  _const_spec((D, 1)), _const_spec((1, LANES)), _const_spec((1, LANES)),
                  _const_spec((LANES, LANES)),
                  stab(ROT_HALF, tm), stab(ROT_HALF, tm), stab(tm, LANES), stab(tm, LANES), stab(tm, LANES),
                  _const_spec((LANES, GLA_QK_W)), _const_spec((1, GLA_QK_W))],
        out_specs=[feat(NSA_Q_W), pl.BlockSpec((1, 2, kvw, tm), lambda s, b: (b, 0, 0, s)), feat(GATE_ROWS),
                   pl.BlockSpec((1, 2, tm, 2 * LANES), lambda s, b: (b, 0, s, 0)),
                   tok(kvw), tok(kvw), tok(GLA_QK_W), tok(GLA_QK_W), tok(GLA_QK_W),
                   tok(GLA_V_W), tok(GLA_V_W)],
        out_shape=[jax.ShapeDtypeStruct((B, NSA_Q_W, S), BF16), jax.ShapeDtypeStruct((B, 2, kvw, S), BF16),
                   jax.ShapeDtypeStruct((B, GATE_ROWS, S), F32),
                   jax.ShapeDtypeStruct((B, 2, S, 2 * LANES), BF16),
                   jax.ShapeDtypeStruct((B * S, kvw), F32), jax.ShapeDtypeStruct((B * S, kvw), F32),
                   jax.ShapeDtypeStruct((B * S, GLA_QK_W), F32), jax.ShapeDtypeStruct((B * S, GLA_QK_W), F32),
                   jax.ShapeDtypeStruct((B * S, GLA_QK_W), F32),
                   jax.ShapeDtypeStruct((B * S, GLA_V_W), BF16), jax.ShapeDtypeStruct((B * S, GLA_V_W), BF16)],
        compiler_params=pltpu.CompilerParams(dimension_semantics=("arbitrary", "arbitrary"),
                                             vmem_limit_bytes=VMEM_LIMIT),
        name="in_proj",
    )(x2, attn_norm.reshape(1, DM), wf, wt, q_gain.reshape(D, 1), tile2(k_gains[1]), tile2(k_gains[2]),
      jnp.asarray(bd, BF16), cosT, sinT, jnp.asarray(rc), jnp.asarray(ra), jnp.asarray(rb),
      wg2, b_gate.reshape(1, GLA_QK_W))
    qT, vcat, gateT, kcat, kc, vc, gq, gk, la, gv, gout = outs

    seq = lambda a: a.reshape(B, S, kvw)
    pad_w2 = lambda w: jnp.concatenate([w, jnp.zeros((CMP_HIDDEN, LANES - D), w.dtype)], axis=1).astype(BF16)
    pe_rows = lambda p: jnp.tile(p.reshape(1, CMP_BLOCK * D), (SUBLANES, 1)).astype(BF16)
    cin = CMP_BLOCK * D
    kcmp, vcT = pl.pallas_call(
        functools.partial(_compress_kernel, nc=nc),
        grid=(B,),
        in_specs=[pl.BlockSpec((1, S, kvw), lambda b: (b, 0, 0)), pl.BlockSpec((1, S, kvw), lambda b: (b, 0, 0)),
                  _const_spec((cin, CMP_HIDDEN)), _const_spec((CMP_HIDDEN, LANES)),
                  _const_spec((cin, CMP_HIDDEN)), _const_spec((CMP_HIDDEN, LANES)),
                  _const_spec((SUBLANES, cin)), _const_spec((SUBLANES, cin)), _const_spec((1, LANES)),
                  _const_spec((nc, LANES)), _const_spec((nc, LANES)), _const_spec((nc, LANES))],
        out_specs=[pl.BlockSpec((1, G, nc, LANES), lambda b: (b, 0, 0, 0)),
                   pl.BlockSpec((1, G, D, nc), lambda b: (b, 0, 0, 0))],
        out_shape=[jax.ShapeDtypeStruct((B, G, nc, LANES), BF16), jax.ShapeDtypeStruct((B, G, D, nc), BF16)],
        compiler_params=pltpu.CompilerParams(dimension_semantics=("arbitrary",), vmem_limit_bytes=VMEM_LIMIT),
        name="compress",
    )(seq(kc), seq(vc), w1_k.astype(BF16), pad_w2(w2_k), w1_v.astype(BF16), pad_w2(w2_v), pe_rows(pe_k), pe_rows(pe_v),
      tile2(k_gains[0]), jnp.asarray(cc), jnp.asarray(ca), jnp.asarray(cb))

    ci = np.arange(nc)[None, :]
    sj = np.arange(ns)[:, None]
    ovl = ((ci * CMP_STRIDE < (sj + 1) * SLC_BLOCK) & (ci * CMP_STRIDE + CMP_BLOCK > sj * SLC_BLOCK)
           & (ci < n_cmp)).astype(np.float32)
    n_q = S // tq
    wq = NSA_HEADS * tq
    o_nsa = pl.pallas_call(
        functools.partial(_nsa_kernel, tq_size=tq, nc=nc, n_cmp=n_cmp, ns=ns, top_k=top_k),
        grid=(B, n_q),
        in_specs=[pl.BlockSpec((1, NSA_Q_W, tq), lambda b, i: (b, 0, i)),
                  pl.BlockSpec((1, G, nc, LANES), lambda b, i: (b, 0, 0, 0)),
                  pl.BlockSpec((1, G, D, nc), lambda b, i: (b, 0, 0, 0)),
                  _const_spec((ns, nc)),
                  pl.BlockSpec((1, 2, S, 2 * LANES), lambda b, i: (b, 0, 0, 0)),
                  pl.BlockSpec((1, 2, kvw, S), lambda b, i: (b, 0, 0, 0)),
                  pl.BlockSpec((1, GATE_ROWS, tq), lambda b, i: (b, 0, i))],
        out_specs=pl.BlockSpec((1, tq, NSA_Q_W), lambda b, i: (b, i, 0)),
        out_shape=jax.ShapeDtypeStruct((B, S, NSA_Q_W), BF16),
        scratch_shapes=[pltpu.VMEM((NSA_HEADS, 2 * D + LANES, tq), BF16),
                        pltpu.VMEM((NSA_HEADS, tq, tq), BF16),
                        pltpu.VMEM((NSA_HEADS, 1, tq), F32),
                        pltpu.VMEM((2, NSA_HEADS, D + 2 * SUBLANES, tq), F32),
                        pltpu.VMEM((2, NSA_HEADS, 1, tq), F32),
                        pltpu.VMEM((NSA_HEADS, D, tq), F32)],
        compiler_params=pltpu.CompilerParams(dimension_semantics=("arbitrary", "arbitrary"),
                                             vmem_limit_bytes=VMEM_LIMIT),
        name="nsa_attn",
    )(qT, kcmp, vcT, jnp.asarray(ovl, BF16), kcat, vcat, gateT)

    g3 = lambda a: a.reshape(B, S, a.shape[-1])
    gspec = lambda w: pl.BlockSpec((1, tg, w), lambda b, t: (b, t, 0))
    hmid = pl.pallas_call(
        functools.partial(_gla_out_kernel, tg=tg),
        grid=(B, S // tg),
        in_specs=[gspec(GLA_QK_W), gspec(GLA_QK_W), gspec(GLA_QK_W), gspec(GLA_V_W), gspec(GLA_V_W),
                  _const_spec((1, GLA_DV)), gspec(DM), gspec(NSA_Q_W), _const_spec((NSA_Q_W + GLA_V_W, DM))],
        out_specs=gspec(DM),
        out_shape=jax.ShapeDtypeStruct((B, S, DM), F32),
        scratch_shapes=[pltpu.VMEM((GLA_HEADS, GLA_DV, GLA_QK_W), F32)],
        compiler_params=pltpu.CompilerParams(dimension_semantics=("arbitrary", "arbitrary"),
                                             vmem_limit_bytes=VMEM_LIMIT),
        name="gla_out",
    )(g3(gq), g3(gk), g3(la), g3(gv), g3(gout), out_gain.reshape(1, GLA_DV), x, o_nsa, w_out.astype(BF16))
    hmid = hmid.reshape(B * S, DM)
    n_t = (B * S) // tm
    row = lambda w: pl.BlockSpec((tm, w), lambda t: (t, 0))

    tf = 256
    assert D_FF % tf == 0
    halo_blocks = tm // SUBLANES
    out = pl.pallas_call(
        functools.partial(_ffn_kernel, tm=tm, tf=tf, tiles_per_seq=n_s),
        grid=(n_t,),
        in_specs=[row(DM),
                  pl.BlockSpec((SUBLANES, DM), lambda t: (jnp.maximum(t * halo_blocks - 1, 0), 0)),
                  _const_spec((1, DM)),
                  pl.BlockSpec((DM, 2 * D_FF), lambda t: (0, 0), pipeline_mode=pl.Buffered(1)),
                  _const_spec((CONV_W, 2 * D_FF)), _const_spec((1, 2 * D_FF)),
                  pl.BlockSpec((D_FF, DM), lambda t: (0, 0), pipeline_mode=pl.Buffered(1))],
        out_specs=row(DM),
        out_shape=jax.ShapeDtypeStruct((B * S, DM), F32),
        scratch_shapes=[pltpu.VMEM((4, tm + SUBLANES, tf), F32), pltpu.VMEM((tm, D_FF), BF16)],
        compiler_params=pltpu.CompilerParams(dimension_semantics=("arbitrary",), vmem_limit_bytes=VMEM_LIMIT),
        name="ffn",
    )(hmid, hmid, ffn_norm.reshape(1, DM), w_up.astype(BF16), conv_w, conv_b.reshape(1, 2 * D_FF),
      w_down.astype(BF16))
    return out.reshape(B, S, DM)
```

```python
import functools

import numpy as np
import jax
import jax.numpy as jnp
from jax import lax
from jax.experimental import pallas as pl
from jax.experimental.pallas import tpu as pltpu

F32 = jnp.float32
BF16 = jnp.bfloat16

NSA_HEADS = 8
NSA_KV_GROUPS = 2
NSA_HEAD_DIM = 64
NSA_HPG = NSA_HEADS // NSA_KV_GROUPS
CMP_BLOCK = 32
CMP_STRIDE = 16
CMP_HIDDEN = 256
SLC_BLOCK = 64
SLC_TOPK = 8
WINDOW = 512
FORCE_SCORE = 1e4
GLA_HEADS = 4
GLA_DK = 64
GLA_DV = 128
GLA_GATE_RANK = 16
GLA_TAU = 16.0
GLA_CHUNK = 64
ROPE_THETA = 500000.0
ROT_DIM = NSA_HEAD_DIM // 4
ROT_HALF = ROT_DIM // 2
D_FF = 2816
CONV_W = 3
EPS = 1e-6
NEG = -1e30
LOG2E = 1.4426950408889634

NSA_Q_W = NSA_HEADS * NSA_HEAD_DIM
NSA_KV_W = NSA_KV_GROUPS * NSA_HEAD_DIM
GLA_QK_W = GLA_HEADS * GLA_DK
GLA_V_W = GLA_HEADS * GLA_DV
IN_SPLITS = [NSA_Q_W, 6 * NSA_KV_W, 3 * NSA_HEADS, GLA_QK_W, GLA_QK_W, GLA_V_W, GLA_GATE_RANK, GLA_V_W]

LANES = 128
SUBLANES = 8
GATE_ROWS = 32
SCORE_AHEAD = 1
VMEM_LIMIT = 56 * 1024 * 1024


def _dot(a, b):
    return jnp.dot(a, b, preferred_element_type=F32)


def _dot_nt(a, b):
    return lax.dot_general(a, b, (((1,), (1,)), ((), ())), preferred_element_type=F32)


def _split_dot(m_bf16, x_f32):
    hi = x_f32.astype(BF16)
    lo = (x_f32 - hi.astype(F32)).astype(BF16)
    return _dot(m_bf16, hi) + _dot(m_bf16, lo)


def _split_dot_r(x_f32, m_bf16):
    hi = x_f32.astype(BF16)
    lo = (x_f32 - hi.astype(F32)).astype(BF16)
    return _dot(hi, m_bf16) + _dot(lo, m_bf16)


def _rope_lanes(y, c, a, b):
    return y * c + pltpu.roll(y, LANES - ROT_HALF, 1) * a + pltpu.roll(y, ROT_HALF, 1) * b


def _sigmoid(z):
    return 1.0 / (1.0 + jnp.exp(-z))


def _in_proj_kernel(x_ref, g_ref, wf_ref, wt_ref, qg_ref, kgs_ref, kgw_ref, bd_ref,
                    cosT_ref, sinT_ref, rc_ref, ra_ref, rb_ref, wg2_ref, bg_ref,
                    qT_ref, vcat_ref, gateT_ref, kcat_ref, kc_ref, vc_ref,
                    gq_ref, gk_ref, la_ref, gv_ref, gout_ref):
    tm = x_ref.shape[0]
    x = x_ref[...]
    ms = jnp.mean(x * x, axis=-1, keepdims=True)
    xn = (x * lax.rsqrt(ms + EPS) * g_ref[...]).astype(BF16)

    offs = np.cumsum([0, NSA_KV_W, NSA_KV_W, 2 * NSA_KV_W, GLA_QK_W, GLA_QK_W, GLA_V_W, GLA_V_W, LANES])
    c_ks, c_kw, c_kvc, c_gq, c_gk, c_gv, c_gout, c_glr, c_end = [int(v) for v in offs]

    def seg(a, b):
        return _dot(xn, wt_ref[:, a:b])

    zf = _dot_nt(wf_ref[...], xn)
    z_ks = seg(c_ks, c_kw)
    z_kw = seg(c_kw, c_kvc)
    cosT = cosT_ref[...]
    sinT = sinT_ref[...]
    qg = qg_ref[...]
    scale = NSA_HEAD_DIM ** -0.5 * LOG2E

    def q_head(h):
        blk = zf[h * NSA_HEAD_DIM:(h + 1) * NSA_HEAD_DIM]
        hms = jnp.mean(blk * blk, axis=0, keepdims=True)
        y = blk * lax.rsqrt(hms + EPS) * qg
        y1 = y[0:ROT_HALF]
        y2 = y[ROT_HALF:ROT_DIM]
        r = jnp.concatenate([y1 * cosT - y2 * sinT, y2 * cosT + y1 * sinT, y[ROT_DIM:]], axis=0)
        qT_ref[0, h * NSA_HEAD_DIM:(h + 1) * NSA_HEAD_DIM, :] = (r * scale).astype(BF16)

    gq_ref[...] = seg(c_gq, c_gk)
    for h in range(0, NSA_HEADS // 2):
        q_head(h)
    gk_ref[...] = seg(c_gk, c_gv)
    for h in range(NSA_HEADS // 2, NSA_HEADS):
        q_head(h)
    gv_ref[...] = seg(c_gv, c_gout).astype(BF16)
    o = NSA_Q_W
    vcat_ref[0, 0] = zf[o:o + NSA_KV_W].astype(BF16)
    vcat_ref[0, 1] = zf[o + NSA_KV_W:o + 2 * NSA_KV_W].astype(BF16)
    gateT_ref[0] = _sigmoid(zf[o + 2 * NSA_KV_W:o + 2 * NSA_KV_W + GATE_ROWS])

    bd = bd_ref[...]
    rc, ra, rb = rc_ref[...], ra_ref[...], rb_ref[...]

    def knorm(z, gain):
        gms = _split_dot_r(z * z, bd)
        return _rope_lanes(z * lax.rsqrt(gms + EPS) * gain, rc, ra, rb).astype(BF16)

    pos = pl.program_id(0) * tm + lax.broadcasted_iota(jnp.int32, (tm, LANES), 0)
    lane = lax.broadcasted_iota(jnp.int32, (tm, LANES), 1)
    onehot = jnp.where(pos // SLC_BLOCK == lane, 1.0, 0.0).astype(BF16)
    zl = jnp.zeros((tm, LANES), BF16)
    gout_ref[...] = seg(c_gout, c_glr).astype(BF16)
    kcat_ref[0, 0] = jnp.concatenate([knorm(z_ks, kgs_ref[...]), onehot], axis=1)
    kvc = seg(c_kvc, c_gq)
    kc_ref[...] = kvc[:, 0:NSA_KV_W]
    vc_ref[...] = kvc[:, NSA_KV_W:2 * NSA_KV_W]
    kcat_ref[0, 1] = jnp.concatenate([knorm(z_kw, kgw_ref[...]), zl], axis=1)
    glr = seg(c_glr, c_end).astype(BF16)
    pre = _dot(glr, wg2_ref[...]) + bg_ref[...]
    la_ref[...] = (jnp.minimum(pre, 0.0) - jnp.log1p(jnp.exp(-jnp.abs(pre)))) * (1.0 / GLA_TAU)


def _gelu_tanh(x):
    return 0.5 * x * (1.0 + jnp.tanh(0.7978845608028654 * (x + 0.044715 * (x * x * x))))


def _compress_kernel(kc_ref, vc_ref, w1k_ref, w2k_ref, w1v_ref, w2v_ref, pek_ref, pev_ref, kg_ref,
                     cc_ref, ca_ref, cb_ref, kcmp_ref, vcT_ref, *, nc):
    half = CMP_STRIDE * NSA_HEAD_DIM
    taps = [[src[0, pl.ds(t, nc, stride=CMP_STRIDE), :] for t in range(CMP_STRIDE)] for src in (kc_ref, vc_ref)]
    for j in range(2 * NSA_KV_GROUPS):
        is_k = j < NSA_KV_GROUPS
        g = j % NSA_KV_GROUPS
        w1 = w1k_ref if is_k else w1v_ref
        w2 = w2k_ref if is_k else w2v_ref
        pe = pek_ref if is_k else pev_ref
        lanes = slice(g * NSA_HEAD_DIM, (g + 1) * NSA_HEAD_DIM)
        r = jnp.concatenate([tp[:, lanes] for tp in taps[0 if is_k else 1]],
                            axis=1).astype(BF16)
        p1 = _dot(r, w1[0:half, :])
        p2 = _dot(r, w1[half:2 * half, :])
        cst = _dot(pe[...], w1[...])[0:1]
        hid = p1 + pltpu.roll(p2, nc - 1, 0) + cst
        out = _dot(_gelu_tanh(hid).astype(BF16), w2[...])
        if is_k:
            kms = jnp.sum(out * out, axis=-1, keepdims=True) * (1.0 / NSA_HEAD_DIM)
            y = out * lax.rsqrt(kms + EPS) * kg_ref[...]
            kcmp_ref[0, g] = _rope_lanes(y, cc_ref[...], ca_ref[...], cb_ref[...]).astype(BF16)
        else:
            vcT_ref[0, g] = out.T[0:NSA_HEAD_DIM].astype(BF16)


def _nsa_kernel(qT_ref, kcmp_ref, vcT_ref, ovl_ref, kcat_ref, vcat_ref, gate_ref, o_ref,
                qall_ref, p_ref, al_ref, acc_ref, m_ref, out_ref,
                *, tq_size, nc, n_cmp, ns, top_k):
    TQ = tq_size
    TK = tq_size
    D = NSA_HEAD_DIM
    G = NSA_KV_GROUPS
    GW = NSA_HPG * TQ
    W = NSA_HEADS * TQ
    WIN, SEL = 0, 1
    P_SEL, P_WIN = 0, 1
    n_back = WINDOW // TK
    i = pl.program_id(1)
    t0 = i * TQ

    zeros = jnp.zeros((D, TQ), BF16)
    for gh in range(NSA_HEADS):
        g = gh // NSA_HPG
        qall_ref[gh, g * D:(g + 1) * D, :] = qT_ref[0, gh * D:(gh + 1) * D, :]
        qall_ref[gh, (1 - g) * D:(2 - g) * D, :] = zeros
        qall_ref[gh, 2 * D:2 * D + LANES, :] = jnp.zeros((LANES, TQ), BF16)
    m_ref[...] = jnp.full(m_ref.shape, NEG, F32)
    acc_ref[...] = jnp.zeros(acc_ref.shape, F32)

    row = lax.broadcasted_iota(jnp.int32, (TK, TQ), 0)
    qcol = lax.broadcasted_iota(jnp.int32, (TK, TQ), 1)
    ones_rows = jnp.ones((2 * SUBLANES, TK), BF16)

    n_full = jnp.minimum(i, n_back - 1)
    last = 2 + n_full + i

    def job(j):
        u = j - 3
        is_full = u < n_full
        plane = jnp.where(is_full, P_WIN, P_SEL)
        kv = jnp.where(is_full, i - 1 - u, u - n_full)
        state = jnp.where(is_full, WIN, SEL)
        plane = jnp.where(j == 0, P_WIN, jnp.where(j == 1, P_SEL, jnp.where(j == 2, P_WIN, plane)))
        kv = jnp.where(j < 2, i, jnp.where(j == 2, jnp.maximum(i - n_back, 0), kv))
        state = jnp.where(j == 0, WIN, jnp.where(j == 1, SEL, jnp.where(j == 2, WIN, state)))
        return state, plane, pl.multiple_of(kv * TK, TK)

    def k_tile(j):
        _, plane, k0 = job(j)
        return kcat_ref[0, plane, pl.ds(k0, TK), :]

    def v_aug(j):
        _, plane, k0 = job(j)
        vT = vcat_ref[0, plane, :, pl.ds(k0, TK)]
        return [jnp.concatenate([vT[g * D:(g + 1) * D], ones_rows], axis=0) for g in range(G)]

    def pv_block(state, vaug, c):
        acc_ref[state, c] = al_ref[c] * acc_ref[state, c] + _dot(vaug[c // NSA_HPG], p_ref[c])

    def sm_block(state, c, s_blk, mask):
        for hb in range(TQ // LANES):
            lanes = slice(hb * LANES, (hb + 1) * LANES)
            s = s_blk[:, lanes]
            if mask is not None:
                s = jnp.where(mask[:, lanes], s, NEG)
            m_old = m_ref[state, c, :, lanes]
            m_new = jnp.maximum(m_old, jnp.max(s, axis=0, keepdims=True))
            al_ref[c, :, lanes] = jnp.exp2(m_old - m_new)
            m_ref[state, c, :, lanes] = m_new
            p_ref[c, :, lanes] = jnp.exp2(s - m_new).astype(BF16)

    def scores(kt, c):
        return _dot(kt, qall_ref[c])

    def step(j, s_first, j_next, prev, mask=None):
        state, _, _ = job(j)
        kt = k_tile(j)
        kt_next = k_tile(j_next)
        if prev is not None:
            state_prev, _, _ = job(prev)
            vaug = v_aug(prev)
        pending = list(s_first)
        for c in range(NSA_HEADS):
            s_cur = pending.pop(0)
            ahead = c + SCORE_AHEAD
            pending.append(scores(kt, ahead) if ahead < NSA_HEADS else scores(kt_next, ahead - NSA_HEADS))
            if prev is not None:
                pv_block(state_prev, vaug, c)
            sm_block(state, c, s_cur, mask)
        return tuple(pending)

    kt0 = k_tile(0)
    s_first = tuple(scores(kt0, c) for c in range(SCORE_AHEAD))

    def gate_row(gh, c):
        return gate_ref[0, gh * 3 + c:gh * 3 + c + 1, :]

    tq_g = t0 + (lax.broadcasted_iota(jnp.int32, (1, GW), 1) & (TQ - 1))
    n_idx = lax.broadcasted_iota(jnp.int32, (nc, GW), 0)
    cmask = (n_idx * CMP_STRIDE + (CMP_BLOCK - 1) <= tq_g) & (n_idx < n_cmp)
    zpad = jnp.zeros((D, GW), BF16)
    groups = range(G)
    gcols = [slice(g * GW, (g + 1) * GW) for g in groups]
    q128 = [jnp.concatenate(
        [jnp.concatenate([qT_ref[0, gh * D:(gh + 1) * D, :] for gh in range(g * NSA_HPG, (g + 1) * NSA_HPG)], axis=1),
         zpad], axis=0) for g in groups]
    s = [jnp.where(cmask, _dot(kcmp_ref[0, g], q128[g]), NEG) for g in groups]
    m = [jnp.max(s[g], axis=0, keepdims=True) for g in groups]
    e = [jnp.where(cmask, jnp.exp2(s[g] - m[g]), 0.0) for g in groups]
    l = [jnp.sum(e[g], axis=0, keepdims=True) for g in groups]
    p = [e[g] * jnp.where(l[g] > 0.0, 1.0 / l[g], 0.0) for g in groups]
    o_cmp = [_dot(vcT_ref[0, g], p[g].astype(BF16)) for g in groups]
    for gh in range(NSA_HEADS):
        g, h = divmod(gh, NSA_HPG)
        out_ref[gh] = gate_row(gh, 0) * o_cmp[g][:, h * TQ:(h + 1) * TQ]
    psum = jnp.concatenate(
        [sum(p[g][:, h * TQ:(h + 1) * TQ] for h in range(1, NSA_HPG)) + p[g][:, 0:TQ] for g in groups], axis=1)
    p_slc = _split_dot(ovl_ref[...], psum)
    tq2 = t0 + (lax.broadcasted_iota(jnp.int32, (1, G * TQ), 1) & (TQ - 1))
    j_idx = lax.broadcasted_iota(jnp.int32, (ns, G * TQ), 0)
    cur = tq2 // SLC_BLOCK
    forced = (j_idx == 0) | (j_idx == cur) | (j_idx == cur - 1)
    score = jnp.where(j_idx <= cur, jnp.where(forced, FORCE_SCORE, p_slc), -jnp.inf)
    ranks = [jnp.zeros((ns, G * TQ), F32), jnp.zeros((ns, G * TQ), F32)]
    for a in range(ns):
        sa = score[a:a + 1, :]
        beats = (sa > score) | ((sa == score) & (j_idx > a))
        ranks[a % 2] = ranks[a % 2] + beats.astype(F32)
    selbias = jnp.where(ranks[0] + ranks[1] < top_k, 0.0, NEG).astype(BF16)
    for gh in range(NSA_HEADS):
        g = gh // NSA_HPG
        qall_ref[gh, 2 * D:2 * D + ns, :] = selbias[:, g * TQ:(g + 1) * TQ]

    diag = row <= qcol
    tail = (row > qcol) & (i >= n_back)
    s_first = step(0, s_first, 1, None, diag)
    s_first = step(1, s_first, 2, 0, diag)
    s_first = step(2, s_first, jnp.minimum(3, last), 1, tail)

    n_unmasked = last - 2

    def unmasked_pair(v, s_in):
        j = 3 + 2 * v
        s_mid = step(j, s_in, j + 1, j - 1)
        return step(j + 1, s_mid, jnp.minimum(j + 2, last), j)

    def unmasked_last(_, s_in):
        return step(last, s_in, last, last - 1)

    s_first = lax.fori_loop(0, n_unmasked // 2, unmasked_pair, s_first)
    lax.fori_loop(0, n_unmasked % 2, unmasked_last, s_first)
    state_last, _, _ = job(last)
    vaug_last = v_aug(last)
    for c in range(NSA_HEADS):
        pv_block(state_last, vaug_last, c)

    def branch_out(state, gh, c):
        return gate_row(gh, c) * (acc_ref[state, gh, 0:D, :] * (1.0 / acc_ref[state, gh, D:D + 1, :]))

    stacked = jnp.concatenate(
        [out_ref[gh] + branch_out(SEL, gh, 1) + branch_out(WIN, gh, 2) for gh in range(NSA_HEADS)], axis=0)
    o_ref[0] = stacked.T.astype(BF16)


def _gla_out_kernel(gq_ref, gk_ref, la_ref, gv_ref, gout_ref, gain_ref, x_ref, on_ref, wo_ref, h_ref,
                    st_ref, *, tg):
    L = GLA_CHUNK

    @pl.when(pl.program_id(1) == 0)
    def _():
        st_ref[...] = jnp.zeros_like(st_ref)

    h_nsa = x_ref[0] + _dot(on_ref[0], wo_ref[0:NSA_Q_W, :])
    la = la_ref[0]
    row = lax.broadcasted_iota(jnp.int32, (tg, tg), 0)
    col = lax.broadcasted_iota(jnp.int32, (tg, tg), 1)
    same = (row // L) == (col // L)
    tril = same & (col <= row)
    n_ch = tg // L
    heads = range(GLA_HEADS)
    chunks = [slice(c * L, (c + 1) * L) for c in range(n_ch)]
    vh = [gv_ref[0, :, h * GLA_DV:(h + 1) * GLA_DV] for h in heads]
    vT = [vh[h].astype(F32).T.astype(BF16) for h in heads]
    b = _split_dot(tril.astype(BF16), la)
    bl_rows = [b[(c + 1) * L - 1:(c + 1) * L, :] for c in range(n_ch)]
    bl = jnp.concatenate([jnp.broadcast_to(r, (L, GLA_QK_W)) for r in bl_rows], axis=0)
    gk = gk_ref[0]
    qt = (gq_ref[0] * (GLA_DK ** -0.5) * jnp.exp(b))
    kt = (gk * jnp.exp(-b)).astype(BF16)
    kd = (gk * jnp.exp(bl - b)).astype(BF16)
    dec = [jnp.exp(r) for r in bl_rows]
    lane = lax.broadcasted_iota(jnp.int32, (1, GLA_QK_W), 1)
    gain = gain_ref[...]
    qth = [jnp.where((lane // GLA_DK) == h, qt, 0.0).astype(BF16) for h in heads]
    a = [jnp.where(tril, _dot_nt(qth[h], kt), 0.0).astype(BF16) for h in heads]
    d_st =[[_dot(vT[h][:, rows], kd[rows]) for rows in chunks] for h in heads]
    o_intra = [_dot(a[h], vh[h]) for h in heads]
    st_in = [[None] * n_ch for _ in heads]
    for h in heads:
        st = st_ref[h]
        for c in range(n_ch):
            st_in[h][c] = st.astype(BF16)
            st = st * dec[c] + d_st[h][c]
        st_ref[h] = st
    outs = []
    for h in heads:
        o = o_intra[h] + jnp.concatenate(
            [_dot_nt(qth[h][rows], st_in[h][c]) for c, rows in enumerate(chunks)], axis=0)
        oms = jnp.mean(o * o, axis=-1, keepdims=True)
        y = o * lax.rsqrt(oms + EPS) * gain
        go = gout_ref[0, :, h * GLA_DV:(h + 1) * GLA_DV].astype(F32)
        outs.append((y * (go * _sigmoid(go))).astype(BF16))
    o_gla = jnp.concatenate(outs, axis=1)
    h_ref[0] = h_nsa + _dot(o_gla, wo_ref[NSA_Q_W:NSA_Q_W + GLA_V_W, :])


def _ffn_kernel(h_ref, halo_ref, g_ref, wup_ref, cw_ref, cb_ref, wdn_ref, o_ref, u_ref, act_ref,
                *, tm, tf, tiles_per_seq):
    h = h_ref[...]
    hh = jnp.concatenate([halo_ref[...], h], axis=0)
    ms = jnp.mean(hh * hh, axis=-1, keepdims=True)
    hn = (hh * lax.rsqrt(ms + EPS) * g_ref[...]).astype(BF16)
    halo_keep = jnp.where((pl.program_id(0) % tiles_per_seq) == 0, 0.0, 1.0)
    lo, hi = SUBLANES, tm + SUBLANES

    def conv(col0, slot):
        u = _dot(hn, wup_ref[:, col0:col0 + tf])
        u_ref[slot, lo:hi, :] = u[lo:hi]
        u_ref[slot, 0:lo, :] = u[0:lo] * halo_keep
        w = cw_ref[:, col0:col0 + tf]
        return (u[lo:hi] * w[2:3] + u_ref[slot, lo - 1:hi - 1, :] * w[1:2]
                + u_ref[slot, lo - 2:hi - 2, :] * w[0:1] + cb_ref[:, col0:col0 + tf])

    for c in range(D_FF // tf):
        gate = conv(c * tf, 2 * (c % 2))
        up = conv(D_FF + c * tf, 2 * (c % 2) + 1)
        act_ref[:, c * tf:(c + 1) * tf] = (gate * _sigmoid(gate) * up).astype(BF16)
    o_ref[...] = h + _dot(act_ref[...], wdn_ref[...])


def _rope_tables(pos):
    inv = ROPE_THETA ** (-np.arange(ROT_HALF, dtype=np.float32) / ROT_HALF)
    ang = pos.astype(np.float32)[:, None] * inv[None, :]
    cos, sin = np.cos(ang), np.sin(ang)
    n = len(pos)
    c = np.ones((n, LANES), np.float32)
    a = np.zeros((n, LANES), np.float32)
    b = np.zeros((n, LANES), np.float32)
    for base in range(0, LANES, NSA_HEAD_DIM):
        c[:, base:base + ROT_HALF] = cos
        c[:, base + ROT_HALF:base + ROT_DIM] = cos
        a[:, base:base + ROT_HALF] = -sin
        b[:, base + ROT_HALF:base + ROT_DIM] = sin
    return cos, sin, c, a, b


def _const_spec(shape):
    nd = len(shape)
    return pl.BlockSpec(shape, lambda *_: (0,) * nd)


def kernel(x, attn_norm, w_in, nsa_q_norm, nsa_k_norm, cmp_pos_k, cmp_pos_v, cmp_w1_k, cmp_w2_k,
           cmp_w1_v, cmp_w2_v, gla_w_gate2, gla_b_gate, gla_out_norm, w_out, ffn_norm, w_up,
           conv_w, conv_b, w_down):
    B, S, DM = x.shape
    depth = attn_norm.shape[0]
    h = x
    for l in range(depth):
        h = _layer(h, attn_norm[l], w_in[l], nsa_q_norm[l], nsa_k_norm[l], cmp_pos_k[l], cmp_pos_v[l],
                   cmp_w1_k[l], cmp_w2_k[l], cmp_w1_v[l], cmp_w2_v[l], gla_w_gate2[l], gla_b_gate[l],
                   gla_out_norm[l], w_out[l], ffn_norm[l], w_up[l], conv_w[l], conv_b[l], w_down[l])
    return h


def _layer(x, attn_norm, w_in, q_gain, k_gains, pe_k, pe_v, w1_k, w2_k, w1_v, w2_v, w_gate2, b_gate,
           out_gain, w_out, ffn_norm, w_up, conv_w, conv_b, w_down):
    B, S, DM = x.shape
    G, D = NSA_KV_GROUPS, NSA_HEAD_DIM
    tm = min(512, S)
    tq = min(256, S)
    tg = min(256, S)
    n_s = S // tm
    nc = S // CMP_STRIDE
    n_cmp = (S - CMP_BLOCK) // CMP_STRIDE + 1
    ns = S // SLC_BLOCK
    top_k = min(SLC_TOPK, ns)
    assert S % tm == 0 and S % tq == 0 and WINDOW % tq == 0 and ns <= LANES and nc % SUBLANES == 0

    offs = np.cumsum([0] + IN_SPLITS)
    nq0, nkv0, ngate0, gq0, gk0, gv0, glr0, gout0 = [int(o) for o in offs[:-1]]
    kvw = NSA_KV_W
    cols = lambda a, n: w_in[:, a:a + n]
    w_feat = jnp.concatenate([cols(nq0, NSA_Q_W), cols(nkv0 + 3 * kvw, kvw), cols(nkv0 + 5 * kvw, kvw),
                              cols(ngate0, 3 * NSA_HEADS),
                              jnp.zeros((DM, GATE_ROWS - 3 * NSA_HEADS), w_in.dtype)], axis=1)
    wf = w_feat.T.astype(BF16)
    w_tok = jnp.concatenate([cols(nkv0 + 2 * kvw, kvw), cols(nkv0 + 4 * kvw, kvw),
                             cols(nkv0, kvw), cols(nkv0 + kvw, kvw),
                             cols(gq0, GLA_QK_W), cols(gk0, GLA_QK_W), cols(gv0, GLA_V_W),
                             cols(gout0, GLA_V_W), cols(glr0, GLA_GATE_RANK),
                             jnp.zeros((DM, LANES - GLA_GATE_RANK), w_in.dtype)], axis=1)
    wt = w_tok.astype(BF16)
    nf, nt = wf.shape[0], wt.shape[1]
    wg2 = jnp.concatenate([w_gate2, jnp.zeros((LANES - GLA_GATE_RANK, GLA_QK_W), w_gate2.dtype)],
                          axis=0).astype(BF16)
    cos, sin, rc, ra, rb = _rope_tables(np.arange(S))
    cosT, sinT = jnp.asarray(cos.T), jnp.asarray(sin.T)
    cmp_end = np.arange(nc) * CMP_STRIDE + CMP_BLOCK - 1
    _, _, cc, ca, cb = _rope_tables(cmp_end)
    bd = np.kron(np.eye(LANES // D, dtype=np.float32), np.full((D, D), 1.0 / D, np.float32))
    tile2 = lambda v: jnp.tile(v.reshape(1, D), (1, LANES // D))

    x2 = x.reshape(B * S, DM)
    tok = lambda w: pl.BlockSpec((tm, w), lambda s, b: (b * n_s + s, 0))
    feat = lambda r: pl.BlockSpec((1, r, tm), lambda s, b: (b, 0, s))
    stab = lambda r, w: pl.BlockSpec((r, w), lambda s, b: (0, s)) if r == ROT_HALF else \
        pl.BlockSpec((r, w), lambda s, b: (s, 0))
    outs = pl.pallas_call(
        _in_proj_kernel,
        grid=(n_s, B),
        in_specs=[tok(DM), _const_spec((1, DM)), _const_spec((nf, DM)), _const_spec((DM, nt)),
                  _const_spec((D, 1)), _const_spec((1, LANES)), _const_spec((1, LANES)),
                  _const_spec((LANES, LANES)),
                  stab(ROT_HALF, tm), stab(ROT_HALF, tm), stab(tm, LANES), stab(tm, LANES), stab(tm, LANES),
                  _const_spec((LANES, GLA_QK_W)), _const_spec((1, GLA_QK_W))],
        out_specs=[feat(NSA_Q_W), pl.BlockSpec((1, 2, kvw, tm), lambda s, b: (b, 0, 0, s)), feat(GATE_ROWS),
                   pl.BlockSpec((1, 2, tm, 2 * LANES), lambda s, b: (b, 0, s, 0)),
                   tok(kvw), tok(kvw), tok(GLA_QK_W), tok(GLA_QK_W), tok(GLA_QK_W),
                   tok(GLA_V_W), tok(GLA_V_W)],
        out_shape=[jax.ShapeDtypeStruct((B, NSA_Q_W, S), BF16), jax.ShapeDtypeStruct((B, 2, kvw, S), BF16),
                   jax.ShapeDtypeStruct((B, GATE_ROWS, S), F32),
                   jax.ShapeDtypeStruct((B, 2, S, 2 * LANES), BF16),
                   jax.ShapeDtypeStruct((B * S, kvw), F32), jax.ShapeDtypeStruct((B * S, kvw), F32),
                   jax.ShapeDtypeStruct((B * S, GLA_QK_W), F32), jax.ShapeDtypeStruct((B * S, GLA_QK_W), F32),
                   jax.ShapeDtypeStruct((B * S, GLA_QK_W), F32),
                   jax.ShapeDtypeStruct((B * S, GLA_V_W), BF16), jax.ShapeDtypeStruct((B * S, GLA_V_W), BF16)],
        compiler_params=pltpu.CompilerParams(dimension_semantics=("arbitrary", "arbitrary"),
                                             vmem_limit_bytes=VMEM_LIMIT),
        name="in_proj",
    )(x2, attn_norm.reshape(1, DM), wf, wt, q_gain.reshape(D, 1), tile2(k_gains[1]), tile2(k_gains[2]),
      jnp.asarray(bd, BF16), cosT, sinT, jnp.asarray(rc), jnp.asarray(ra), jnp.asarray(rb),
      wg2, b_gate.reshape(1, GLA_QK_W))
    qT, vcat, gateT, kcat, kc, vc, gq, gk, la, gv, gout = outs

    seq = lambda a: a.reshape(B, S, kvw)
    pad_w2 = lambda w: jnp.concatenate([w, jnp.zeros((CMP_HIDDEN, LANES - D), w.dtype)], axis=1).astype(BF16)
    pe_rows = lambda p: jnp.tile(p.reshape(1, CMP_BLOCK * D), (SUBLANES, 1)).astype(BF16)
    cin = CMP_BLOCK * D
    kcmp, vcT = pl.pallas_call(
        functools.partial(_compress_kernel, nc=nc),
        grid=(B,),
        in_specs=[pl.BlockSpec((1, S, kvw), lambda b: (b, 0, 0)), pl.BlockSpec((1, S, kvw), lambda b: (b, 0, 0)),
                  _const_spec((cin, CMP_HIDDEN)), _const_spec((CMP_HIDDEN, LANES)),
                  _const_spec((cin, CMP_HIDDEN)), _const_spec((CMP_HIDDEN, LANES)),
                  _const_spec((SUBLANES, cin)), _const_spec((SUBLANES, cin)), _const_spec((1, LANES)),
                  _const_spec((nc, LANES)), _const_spec((nc, LANES)), _const_spec((nc, LANES))],
        out_specs=[pl.BlockSpec((1, G, nc, LANES), lambda b: (b, 0, 0, 0)),
                   pl.BlockSpec((1, G, D, nc), lambda b: (b, 0, 0, 0))],
        out_shape=[jax.ShapeDtypeStruct((B, G, nc, LANES), BF16), jax.ShapeDtypeStruct((B, G, D, nc), BF16)],
        compiler_params=pltpu.CompilerParams(dimension_semantics=("arbitrary",), vmem_limit_bytes=VMEM_LIMIT),
        name="compress",
    )(seq(kc), seq(vc), w1_k.astype(BF16), pad_w2(w2_k), w1_v.astype(BF16), pad_w2(w2_v), pe_rows(pe_k), pe_rows(pe_v),
      tile2(k_gains[0]), jnp.asarray(cc), jnp.asarray(ca), jnp.asarray(cb))

    ci = np.arange(nc)[None, :]
    sj = np.arange(ns)[:, None]
    ovl = ((ci * CMP_STRIDE < (sj + 1) * SLC_BLOCK) & (ci * CMP_STRIDE + CMP_BLOCK > sj * SLC_BLOCK)
           & (ci < n_cmp)).astype(np.float32)
    n_q = S // tq
    wq = NSA_HEADS * tq
    o_nsa = pl.pallas_call(
        functools.partial(_nsa_kernel, tq_size=tq, nc=nc, n_cmp=n_cmp, ns=ns, top_k=top_k),
        grid=(B, n_q),
        in_specs=[pl.BlockSpec((1, NSA_Q_W, tq), lambda b, i: (b, 0, i)),
                  pl.BlockSpec((1, G, nc, LANES), lambda b, i: (b, 0, 0, 0)),
                  pl.BlockSpec((1, G, D, nc), lambda b, i: (b, 0, 0, 0)),
                  _const_spec((ns, nc)),
                  pl.BlockSpec((1, 2, S, 2 * LANES), lambda b, i: (b, 0, 0, 0)),
                  pl.BlockSpec((1, 2, kvw, S), lambda b, i: (b, 0, 0, 0)),
                  pl.BlockSpec((1, GATE_ROWS, tq), lambda b, i: (b, 0, i))],
        out_specs=pl.BlockSpec((1, tq, NSA_Q_W), lambda b, i: (b, i, 0)),
        out_shape=jax.ShapeDtypeStruct((B, S, NSA_Q_W), BF16),
        scratch_shapes=[pltpu.VMEM((NSA_HEADS, 2 * D + LANES, tq), BF16),
                        pltpu.VMEM((NSA_HEADS, tq, tq), BF16),
                        pltpu.VMEM((NSA_HEADS, 1, tq), F32),
                        pltpu.VMEM((2, NSA_HEADS, D + 2 * SUBLANES, tq), F32),
                        pltpu.VMEM((2, NSA_HEADS, 1, tq), F32),
                        pltpu.VMEM((NSA_HEADS, D, tq), F32)],
        compiler_params=pltpu.CompilerParams(dimension_semantics=("arbitrary", "arbitrary"),
                                             vmem_limit_bytes=VMEM_LIMIT),
        name="nsa_attn",
    )(qT, kcmp, vcT, jnp.asarray(ovl, BF16), kcat, vcat, gateT)

    g3 = lambda a: a.reshape(B, S, a.shape[-1])
    gspec = lambda w: pl.BlockSpec((1, tg, w), lambda b, t: (b, t, 0))
    hmid = pl.pallas_call(
        functools.partial(_gla_out_kernel, tg=tg),
        grid=(B, S // tg),
        in_specs=[gspec(GLA_QK_W), gspec(GLA_QK_W), gspec(GLA_QK_W), gspec(GLA_V_W), gspec(GLA_V_W),
                  _const_spec((1, GLA_DV)), gspec(DM), gspec(NSA_Q_W), _const_spec((NSA_Q_W + GLA_V_W, DM))],
        out_specs=gspec(DM),
        out_shape=jax.ShapeDtypeStruct((B, S, DM), F32),
        scratch_shapes=[pltpu.VMEM((GLA_HEADS, GLA_DV, GLA_QK_W), F32)],
        compiler_params=pltpu.CompilerParams(dimension_semantics=("arbitrary", "arbitrary"),
                                             vmem_limit_bytes=VMEM_LIMIT),
        name="gla_out",
    )(g3(gq), g3(gk), g3(la), g3(gv), g3(gout), out_gain.reshape(1, GLA_DV), x, o_nsa, w_out.astype(BF16))
    hmid = hmid.reshape(B * S, DM)
    tmf = min(1024, S)
    n_t = (B * S) // tmf
    row = lambda w: pl.BlockSpec((tmf, w), lambda t: (t, 0))

    tf = 256
    assert D_FF % tf == 0
    halo_blocks = tmf // SUBLANES
    out = pl.pallas_call(
        functools.partial(_ffn_kernel, tm=tmf, tf=tf, tiles_per_seq=S // tmf),
        grid=(n_t,),
        in_specs=[row(DM),
                  pl.BlockSpec((SUBLANES, DM), lambda t: (jnp.maximum(t * halo_blocks - 1, 0), 0)),
                  _const_spec((1, DM)),
                  pl.BlockSpec((DM, 2 * D_FF), lambda t: (0, 0), pipeline_mode=pl.Buffered(1)),
                  _const_spec((CONV_W, 2 * D_FF)), _const_spec((1, 2 * D_FF)),
                  pl.BlockSpec((D_FF, DM), lambda t: (0, 0), pipeline_mode=pl.Buffered(1))],
        out_specs=row(DM),
        out_shape=jax.ShapeDtypeStruct((B * S, DM), F32),
        scratch_shapes=[pltpu.VMEM((4, tmf + SUBLANES, tf), F32), pltpu.VMEM((tmf, D_FF), BF16)],
        compiler_params=pltpu.CompilerParams(dimension_semantics=("arbitrary",), vmem_limit_bytes=VMEM_LIMIT),
        name="ffn",
    )(hmid, hmid, ffn_norm.reshape(1, DM), w_up.astype(BF16), conv_w, conv_b.reshape(1, 2 * D_FF),
      w_down.astype(BF16))
    return out.reshape(B, S, DM)
```

```python
import functools

import numpy as np
import jax
import jax.numpy as jnp
from jax import lax
from jax.experimental import pallas as pl
from jax.experimental.pallas import tpu as pltpu

F32 = jnp.float32
BF16 = jnp.bfloat16

NSA_HEADS = 8
NSA_KV_GROUPS = 2
NSA_HEAD_DIM = 64
NSA_HPG = NSA_HEADS // NSA_KV_GROUPS
CMP_BLOCK = 32
CMP_STRIDE = 16
CMP_HIDDEN = 256
SLC_BLOCK = 64
SLC_TOPK = 8
WINDOW = 512
FORCE_SCORE = 1e4
GLA_HEADS = 4
GLA_DK = 64
GLA_DV = 128
GLA_GATE_RANK = 16
GLA_TAU = 16.0
GLA_CHUNK = 64
ROPE_THETA = 500000.0
ROT_DIM = NSA_HEAD_DIM // 4
ROT_HALF = ROT_DIM // 2
D_FF = 2816
CONV_W = 3
EPS = 1e-6
NEG = -1e30
LOG2E = 1.4426950408889634

NSA_Q_W = NSA_HEADS * NSA_HEAD_DIM
NSA_KV_W = NSA_KV_GROUPS * NSA_HEAD_DIM
GLA_QK_W = GLA_HEADS * GLA_DK
GLA_V_W = GLA_HEADS * GLA_DV
IN_SPLITS = [NSA_Q_W, 6 * NSA_KV_W, 3 * NSA_HEADS, GLA_QK_W, GLA_QK_W, GLA_V_W, GLA_GATE_RANK, GLA_V_W]

LANES = 128
SUBLANES = 8
GATE_ROWS = 32
SCORE_AHEAD = 1
VMEM_LIMIT = 56 * 1024 * 1024


def _dot(a, b):
    return jnp.dot(a, b, preferred_element_type=F32)


def _dot_nt(a, b):
    return lax.dot_general(a, b, (((1,), (1,)), ((), ())), preferred_element_type=F32)


def _split_dot(m_bf16, x_f32):
    hi = x_f32.astype(BF16)
    lo = (x_f32 - hi.astype(F32)).astype(BF16)
    return _dot(m_bf16, hi) + _dot(m_bf16, lo)


def _split_dot_r(x_f32, m_bf16):
    hi = x_f32.astype(BF16)
    lo = (x_f32 - hi.astype(F32)).astype(BF16)
    return _dot(hi, m_bf16) + _dot(lo, m_bf16)


def _rope_lanes(y, c, a, b):
    return y * c + pltpu.roll(y, LANES - ROT_HALF, 1) * a + pltpu.roll(y, ROT_HALF, 1) * b


def _sigmoid(z):
    return 1.0 / (1.0 + jnp.exp(-z))


def _in_proj_kernel(x_ref, g_ref, wf_ref, wt_ref, qg_ref, kgs_ref, kgw_ref, bd_ref,
                    cosT_ref, sinT_ref, rc_ref, ra_ref, rb_ref, wg2_ref, bg_ref,
                    qT_ref, vcat_ref, gateT_ref, kcat_ref, kc_ref, vc_ref,
                    gq_ref, gk_ref, la_ref, gv_ref, gout_ref):
    tm = x_ref.shape[0]
    x = x_ref[...]
    ms = jnp.mean(x * x, axis=-1, keepdims=True)
    xn = (x * lax.rsqrt(ms + EPS) * g_ref[...]).astype(BF16)

    offs = np.cumsum([0, NSA_KV_W, NSA_KV_W, 2 * NSA_KV_W, GLA_QK_W, GLA_QK_W, GLA_V_W, GLA_V_W, LANES])
    c_ks, c_kw, c_kvc, c_gq, c_gk, c_gv, c_gout, c_glr, c_end = [int(v) for v in offs]

    def seg(a, b):
        return _dot(xn, wt_ref[:, a:b])

    zf = _dot_nt(wf_ref[...], xn)
    z_ks = seg(c_ks, c_kw)
    z_kw = seg(c_kw, c_kvc)
    cosT = cosT_ref[...]
    sinT = sinT_ref[...]
    qg = qg_ref[...]
    scale = NSA_HEAD_DIM ** -0.5 * LOG2E

    def q_head(h):
        blk = zf[h * NSA_HEAD_DIM:(h + 1) * NSA_HEAD_DIM]
        hms = jnp.mean(blk * blk, axis=0, keepdims=True)
        y = blk * lax.rsqrt(hms + EPS) * qg
        y1 = y[0:ROT_HALF]
        y2 = y[ROT_HALF:ROT_DIM]
        r = jnp.concatenate([y1 * cosT - y2 * sinT, y2 * cosT + y1 * sinT, y[ROT_DIM:]], axis=0)
        qT_ref[0, h * NSA_HEAD_DIM:(h + 1) * NSA_HEAD_DIM, :] = (r * scale).astype(BF16)

    gq_ref[...] = seg(c_gq, c_gk)
    for h in range(0, NSA_HEADS // 2):
        q_head(h)
    gk_ref[...] = seg(c_gk, c_gv)
    for h in range(NSA_HEADS // 2, NSA_HEADS):
        q_head(h)
    gv_ref[...] = seg(c_gv, c_gout).astype(BF16)
    o = NSA_Q_W
    vcat_ref[0, 0] = zf[o:o + NSA_KV_W].astype(BF16)
    vcat_ref[0, 1] = zf[o + NSA_KV_W:o + 2 * NSA_KV_W].astype(BF16)
    gateT_ref[0] = _sigmoid(zf[o + 2 * NSA_KV_W:o + 2 * NSA_KV_W + GATE_ROWS])

    bd = bd_ref[...]
    rc, ra, rb = rc_ref[...], ra_ref[...], rb_ref[...]

    def knorm(z, gain):
        gms = _split_dot_r(z * z, bd)
        return _rope_lanes(z * lax.rsqrt(gms + EPS) * gain, rc, ra, rb).astype(BF16)

    pos = pl.program_id(0) * tm + lax.broadcasted_iota(jnp.int32, (tm, LANES), 0)
    lane = lax.broadcasted_iota(jnp.int32, (tm, LANES), 1)
    onehot = jnp.where(pos // SLC_BLOCK == lane, 1.0, 0.0).astype(BF16)
    zl = jnp.zeros((tm, LANES), BF16)
    gout_ref[...] = seg(c_gout, c_glr).astype(BF16)
    kcat_ref[0, 0] = jnp.concatenate([knorm(z_ks, kgs_ref[...]), onehot], axis=1)
    kvc = seg(c_kvc, c_gq)
    kc_ref[...] = kvc[:, 0:NSA_KV_W]
    vc_ref[...] = kvc[:, NSA_KV_W:2 * NSA_KV_W]
    kcat_ref[0, 1] = jnp.concatenate([knorm(z_kw, kgw_ref[...]), zl], axis=1)
    glr = seg(c_glr, c_end).astype(BF16)
    pre = _dot(glr, wg2_ref[...]) + bg_ref[...]
    la_ref[...] = (jnp.minimum(pre, 0.0) - jnp.log1p(jnp.exp(-jnp.abs(pre)))) * (1.0 / GLA_TAU)


def _gelu_tanh(x):
    return 0.5 * x * (1.0 + jnp.tanh(0.7978845608028654 * (x + 0.044715 * (x * x * x))))


def _compress_kernel(kc_ref, vc_ref, w1k_ref, w2k_ref, w1v_ref, w2v_ref, pek_ref, pev_ref, kg_ref,
                     cc_ref, ca_ref, cb_ref, kcmp_ref, vcT_ref, *, nc):
    half = CMP_STRIDE * NSA_HEAD_DIM
    taps = [[src[0, pl.ds(t, nc, stride=CMP_STRIDE), :] for t in range(CMP_STRIDE)] for src in (kc_ref, vc_ref)]
    for j in range(2 * NSA_KV_GROUPS):
        is_k = j < NSA_KV_GROUPS
        g = j % NSA_KV_GROUPS
        w1 = w1k_ref if is_k else w1v_ref
        w2 = w2k_ref if is_k else w2v_ref
        pe = pek_ref if is_k else pev_ref
        lanes = slice(g * NSA_HEAD_DIM, (g + 1) * NSA_HEAD_DIM)
        r = jnp.concatenate([tp[:, lanes] for tp in taps[0 if is_k else 1]],
                            axis=1).astype(BF16)
        p1 = _dot(r, w1[0:half, :])
        p2 = _dot(r, w1[half:2 * half, :])
        cst = _dot(pe[...], w1[...])[0:1]
        hid = p1 + pltpu.roll(p2, nc - 1, 0) + cst
        out = _dot(_gelu_tanh(hid).astype(BF16), w2[...])
        if is_k:
            kms = jnp.sum(out * out, axis=-1, keepdims=True) * (1.0 / NSA_HEAD_DIM)
            y = out * lax.rsqrt(kms + EPS) * kg_ref[...]
            kcmp_ref[0, g] = _rope_lanes(y, cc_ref[...], ca_ref[...], cb_ref[...]).astype(BF16)
        else:
            vcT_ref[0, g] = out.T[0:NSA_HEAD_DIM].astype(BF16)


def _nsa_kernel(qT_ref, kcmp_ref, vcT_ref, ovl_ref, kcat_ref, vcat_ref, gate_ref, o_ref,
                qall_ref, p_ref, al_ref, acc_ref, m_ref, out_ref,
                *, tq_size, nc, n_cmp, ns, top_k):
    TQ = tq_size
    TK = tq_size
    D = NSA_HEAD_DIM
    G = NSA_KV_GROUPS
    GW = NSA_HPG * TQ
    W = NSA_HEADS * TQ
    WIN, SEL = 0, 1
    P_SEL, P_WIN = 0, 1
    n_back = WINDOW // TK
    i = pl.program_id(1)
    t0 = i * TQ

    zeros = jnp.zeros((D, TQ), BF16)
    for gh in range(NSA_HEADS):
        g = gh // NSA_HPG
        qall_ref[gh, g * D:(g + 1) * D, :] = qT_ref[0, gh * D:(gh + 1) * D, :]
        qall_ref[gh, (1 - g) * D:(2 - g) * D, :] = zeros
        qall_ref[gh, 2 * D:2 * D + LANES, :] = jnp.zeros((LANES, TQ), BF16)
    m_ref[...] = jnp.full(m_ref.shape, NEG, F32)
    acc_ref[...] = jnp.zeros(acc_ref.shape, F32)

    row = lax.broadcasted_iota(jnp.int32, (TK, TQ), 0)
    qcol = lax.broadcasted_iota(jnp.int32, (TK, TQ), 1)
    ones_rows = jnp.ones((2 * SUBLANES, TK), BF16)

    n_full = jnp.minimum(i, n_back - 1)
    last = 2 + n_full + i

    def job(j):
        u = j - 3
        is_full = u < n_full
        plane = jnp.where(is_full, P_WIN, P_SEL)
        kv = jnp.where(is_full, i - 1 - u, u - n_full)
        state = jnp.where(is_full, WIN, SEL)
        plane = jnp.where(j == 0, P_WIN, jnp.where(j == 1, P_SEL, jnp.where(j == 2, P_WIN, plane)))
        kv = jnp.where(j < 2, i, jnp.where(j == 2, jnp.maximum(i - n_back, 0), kv))
        state = jnp.where(j == 0, WIN, jnp.where(j == 1, SEL, jnp.where(j == 2, WIN, state)))
        return state, plane, pl.multiple_of(kv * TK, TK)

    def k_tile(j):
        _, plane, k0 = job(j)
        return kcat_ref[0, plane, pl.ds(k0, TK), :]

    def v_aug(j):
        _, plane, k0 = job(j)
        vT = vcat_ref[0, plane, :, pl.ds(k0, TK)]
        return [jnp.concatenate([vT[g * D:(g + 1) * D], ones_rows], axis=0) for g in range(G)]

    def pv_block(state, vaug, c):
        acc_ref[state, c] = al_ref[c] * acc_ref[state, c] + _dot(vaug[c // NSA_HPG], p_ref[c])

    def sm_block(state, c, s_blk, mask):
        for hb in range(TQ // LANES):
            lanes = slice(hb * LANES, (hb + 1) * LANES)
            s = s_blk[:, lanes]
            if mask is not None:
                s = jnp.where(mask[:, lanes], s, NEG)
            m_old = m_ref[state, c, :, lanes]
            m_new = jnp.maximum(m_old, jnp.max(s, axis=0, keepdims=True))
            al_ref[c, :, lanes] = jnp.exp2(m_old - m_new)
            m_ref[state, c, :, lanes] = m_new
            p_ref[c, :, lanes] = jnp.exp2(s - m_new).astype(BF16)

    def scores(kt, c):
        return _dot(kt, qall_ref[c])

    def step(j, s_first, j_next, prev, mask=None):
        state, _, _ = job(j)
        kt = k_tile(j)
        kt_next = k_tile(j_next)
        if prev is not None:
            state_prev, _, _ = job(prev)
            vaug = v_aug(prev)
        pending = list(s_first)
        for c in range(NSA_HEADS):
            s_cur = pending.pop(0)
            ahead = c + SCORE_AHEAD
            pending.append(scores(kt, ahead) if ahead < NSA_HEADS else scores(kt_next, ahead - NSA_HEADS))
            if prev is not None:
                pv_block(state_prev, vaug, c)
            sm_block(state, c, s_cur, mask)
        return tuple(pending)

    kt0 = k_tile(0)
    s_first = tuple(scores(kt0, c) for c in range(SCORE_AHEAD))

    def gate_row(gh, c):
        return gate_ref[0, gh * 3 + c:gh * 3 + c + 1, :]

    tq_g = t0 + (lax.broadcasted_iota(jnp.int32, (1, GW), 1) & (TQ - 1))
    n_idx = lax.broadcasted_iota(jnp.int32, (nc, GW), 0)
    cmask = (n_idx * CMP_STRIDE + (CMP_BLOCK - 1) <= tq_g) & (n_idx < n_cmp)
    zpad = jnp.zeros((D, GW), BF16)
    groups = range(G)
    gcols = [slice(g * GW, (g + 1) * GW) for g in groups]
    q128 = [jnp.concatenate(
        [jnp.concatenate([qT_ref[0, gh * D:(gh + 1) * D, :] for gh in range(g * NSA_HPG, (g + 1) * NSA_HPG)], axis=1),
         zpad], axis=0) for g in groups]
    s = [jnp.where(cmask, _dot(kcmp_ref[0, g], q128[g]), NEG) for g in groups]
    m = [jnp.max(s[g], axis=0, keepdims=True) for g in groups]
    e = [jnp.where(cmask, jnp.exp2(s[g] - m[g]), 0.0) for g in groups]
    l = [jnp.sum(e[g], axis=0, keepdims=True) for g in groups]
    p = [e[g] * jnp.where(l[g] > 0.0, 1.0 / l[g], 0.0) for g in groups]
    o_cmp = [_dot(vcT_ref[0, g], p[g].astype(BF16)) for g in groups]
    for gh in range(NSA_HEADS):
        g, h = divmod(gh, NSA_HPG)
        out_ref[gh] = gate_row(gh, 0) * o_cmp[g][:, h * TQ:(h + 1) * TQ]
    psum = jnp.concatenate(
        [sum(p[g][:, h * TQ:(h + 1) * TQ] for h in range(1, NSA_HPG)) + p[g][:, 0:TQ] for g in groups], axis=1)
    p_slc = _split_dot(ovl_ref[...], psum)
    tq2 = t0 + (lax.broadcasted_iota(jnp.int32, (1, G * TQ), 1) & (TQ - 1))
    j_idx = lax.broadcasted_iota(jnp.int32, (ns, G * TQ), 0)
    cur = tq2 // SLC_BLOCK
    forced = (j_idx == 0) | (j_idx == cur) | (j_idx == cur - 1)
    score = jnp.where(j_idx <= cur, jnp.where(forced, FORCE_SCORE, p_slc), -jnp.inf)
    ranks = [jnp.zeros((ns, G * TQ), F32), jnp.zeros((ns, G * TQ), F32)]
    for a in range(ns):
        sa = score[a:a + 1, :]
        beats = (sa > score) | ((sa == score) & (j_idx > a))
        ranks[a % 2] = ranks[a % 2] + beats.astype(F32)
    selbias = jnp.where(ranks[0] + ranks[1] < top_k, 0.0, NEG).astype(BF16)
    for gh in range(NSA_HEADS):
        g = gh // NSA_HPG
        qall_ref[gh, 2 * D:2 * D + ns, :] = selbias[:, g * TQ:(g + 1) * TQ]

    diag = row <= qcol
    tail = (row > qcol) & (i >= n_back)
    s_first = step(0, s_first, 1, None, diag)
    s_first = step(1, s_first, 2, 0, diag)
    s_first = step(2, s_first, jnp.minimum(3, last), 1, tail)

    n_unmasked = last - 2

    def unmasked_pair(v, s_in):
        j = 3 + 2 * v
        s_mid = step(j, s_in, j + 1, j - 1)
        return step(j + 1, s_mid, jnp.minimum(j + 2, last), j)

    def unmasked_last(_, s_in):
        return step(last, s_in, last, last - 1)

    s_first = lax.fori_loop(0, n_unmasked // 2, unmasked_pair, s_first)
    lax.fori_loop(0, n_unmasked % 2, unmasked_last, s_first)
    state_last, _, _ = job(last)
    vaug_last = v_aug(last)
    for c in range(NSA_HEADS):
        pv_block(state_last, vaug_last, c)

    def branch_out(state, gh, c):
        return gate_row(gh, c) * (acc_ref[state, gh, 0:D, :] * (1.0 / acc_ref[state, gh, D:D + 1, :]))

    stacked = jnp.concatenate(
        [out_ref[gh] + branch_out(SEL, gh, 1) + branch_out(WIN, gh, 2) for gh in range(NSA_HEADS)], axis=0)
    o_ref[0] = stacked.T.astype(BF16)


def _gla_out_kernel(gq_ref, gk_ref, la_ref, gv_ref, gout_ref, gain_ref, x_ref, on_ref, wo_ref, h_ref,
                    st_ref, *, tg):
    L = GLA_CHUNK

    @pl.when(pl.program_id(1) == 0)
    def _():
        st_ref[...] = jnp.zeros_like(st_ref)

    h_nsa = x_ref[0] + _dot(on_ref[0], wo_ref[0:NSA_Q_W, :])
    la = la_ref[0]
    row = lax.broadcasted_iota(jnp.int32, (tg, tg), 0)
    col = lax.broadcasted_iota(jnp.int32, (tg, tg), 1)
    same = (row // L) == (col // L)
    tril = same & (col <= row)
    n_ch = tg // L
    heads = range(GLA_HEADS)
    chunks = [slice(c * L, (c + 1) * L) for c in range(n_ch)]
    vh = [gv_ref[0, :, h * GLA_DV:(h + 1) * GLA_DV] for h in heads]
    vT = [vh[h].astype(F32).T.astype(BF16) for h in heads]
    b = _split_dot(tril.astype(BF16), la)
    bl_rows = [b[(c + 1) * L - 1:(c + 1) * L, :] for c in range(n_ch)]
    bl = jnp.concatenate([jnp.broadcast_to(r, (L, GLA_QK_W)) for r in bl_rows], axis=0)
    gk = gk_ref[0]
    qt = (gq_ref[0] * (GLA_DK ** -0.5) * jnp.exp(b))
    kt = (gk * jnp.exp(-b)).astype(BF16)
    kd = (gk * jnp.exp(bl - b)).astype(BF16)
    dec = [jnp.exp(r) for r in bl_rows]
    lane = lax.broadcasted_iota(jnp.int32, (1, GLA_QK_W), 1)
    gain = gain_ref[...]
    qth = [jnp.where((lane // GLA_DK) == h, qt, 0.0).astype(BF16) for h in heads]
    a = [jnp.where(tril, _dot_nt(qth[h], kt), 0.0).astype(BF16) for h in heads]
    d_st =[[_dot(vT[h][:, rows], kd[rows]) for rows in chunks] for h in heads]
    o_intra = [_dot(a[h], vh[h]) for h in heads]
    st_in = [[None] * n_ch for _ in heads]
    for h in heads:
        st = st_ref[h]
        for c in range(n_ch):
            st_in[h][c] = st.astype(BF16)
            st = st * dec[c] + d_st[h][c]
        st_ref[h] = st
    outs = []
    for h in heads:
        o = o_intra[h] + jnp.concatenate(
            [_dot_nt(qth[h][rows], st_in[h][c]) for c, rows in enumerate(chunks)], axis=0)
        oms = jnp.mean(o * o, axis=-1, keepdims=True)
        y = o * lax.rsqrt(oms + EPS) * gain
        go = gout_ref[0, :, h * GLA_DV:(h + 1) * GLA_DV].astype(F32)
        outs.append((y * (go * _sigmoid(go))).astype(BF16))
    o_gla = jnp.concatenate(outs, axis=1)
    h_ref[0] = h_nsa + _dot(o_gla, wo_ref[NSA_Q_W:NSA_Q_W + GLA_V_W, :])


def _ffn_kernel(h_ref, halo_ref, g_ref, wup_ref, cw_ref, cb_ref, wdn_ref, o_ref, u_ref, act_ref,
                *, tm, tf, tiles_per_seq):
    h = h_ref[...]
    hh = jnp.concatenate([halo_ref[...], h], axis=0)
    ms = jnp.mean(hh * hh, axis=-1, keepdims=True)
    hn = (hh * lax.rsqrt(ms + EPS) * g_ref[...]).astype(BF16)
    halo_keep = jnp.where((pl.program_id(0) % tiles_per_seq) == 0, 0.0, 1.0)
    lo, hi = SUBLANES, tm + SUBLANES

    def conv(col0, slot):
        u = _dot(hn, wup_ref[:, col0:col0 + tf])
        u_ref[slot, lo:hi, :] = u[lo:hi]
        u_ref[slot, 0:lo, :] = u[0:lo] * halo_keep
        w = cw_ref[:, col0:col0 + tf]
        return (u[lo:hi] * w[2:3] + u_ref[slot, lo - 1:hi - 1, :] * w[1:2]
                + u_ref[slot, lo - 2:hi - 2, :] * w[0:1] + cb_ref[:, col0:col0 + tf])

    for c in range(D_FF // tf):
        gate = conv(c * tf, 2 * (c % 2))
        up = conv(D_FF + c * tf, 2 * (c % 2) + 1)
        act_ref[:, c * tf:(c + 1) * tf] = (gate * _sigmoid(gate) * up).astype(BF16)
    o_ref[...] = h + _dot(act_ref[...], wdn_ref[...])


def _rope_tables(pos):
    inv = ROPE_THETA ** (-np.arange(ROT_HALF, dtype=np.float32) / ROT_HALF)
    ang = pos.astype(np.float32)[:, None] * inv[None, :]
    cos, sin = np.cos(ang), np.sin(ang)
    n = len(pos)
    c = np.ones((n, LANES), np.float32)
    a = np.zeros((n, LANES), np.float32)
    b = np.zeros((n, LANES), np.float32)
    for base in range(0, LANES, NSA_HEAD_DIM):
        c[:, base:base + ROT_HALF] = cos
        c[:, base + ROT_HALF:base + ROT_DIM] = cos
        a[:, base:base + ROT_HALF] = -sin
        b[:, base + ROT_HALF:base + ROT_DIM] = sin
    return cos, sin, c, a, b


def _const_spec(shape):
    nd = len(shape)
    return pl.BlockSpec(shape, lambda *_: (0,) * nd)


def kernel(x, attn_norm, w_in, nsa_q_norm, nsa_k_norm, cmp_pos_k, cmp_pos_v, cmp_w1_k, cmp_w2_k,
           cmp_w1_v, cmp_w2_v, gla_w_gate2, gla_b_gate, gla_out_norm, w_out, ffn_norm, w_up,
           conv_w, conv_b, w_down):
    B, S, DM = x.shape
    depth = attn_norm.shape[0]
    h = x
    for l in range(depth):
        h = _layer(h, attn_norm[l], w_in[l], nsa_q_norm[l], nsa_k_norm[l], cmp_pos_k[l], cmp_pos_v[l],
                   cmp_w1_k[l], cmp_w2_k[l], cmp_w1_v[l], cmp_w2_v[l], gla_w_gate2[l], gla_b_gate[l],
                   gla_out_norm[l], w_out[l], ffn_norm[l], w_up[l], conv_w[l], conv_b[l], w_down[l])
    return h


def _layer(x, attn_norm, w_in, q_gain, k_gains, pe_k, pe_v, w1_k, w2_k, w1_v, w2_v, w_gate2, b_gate,
           out_gain, w_out, ffn_norm, w_up, conv_w, conv_b, w_down):
    B, S, DM = x.shape
    G, D = NSA_KV_GROUPS, NSA_HEAD_DIM
    tm = min(1024, S)
    tq = min(256, S)
    tg = min(256, S)
    n_s = S // tm
    nc = S // CMP_STRIDE
    n_cmp = (S - CMP_BLOCK) // CMP_STRIDE + 1
    ns = S // SLC_BLOCK
    top_k = min(SLC_TOPK, ns)
    assert S % tm == 0 and S % tq == 0 and WINDOW % tq == 0 and ns <= LANES and nc % SUBLANES == 0

    offs = np.cumsum([0] + IN_SPLITS)
    nq0, nkv0, ngate0, gq0, gk0, gv0, glr0, gout0 = [int(o) for o in offs[:-1]]
    kvw = NSA_KV_W
    cols = lambda a, n: w_in[:, a:a + n]
    w_feat = jnp.concatenate([cols(nq0, NSA_Q_W), cols(nkv0 + 3 * kvw, kvw), cols(nkv0 + 5 * kvw, kvw),
                              cols(ngate0, 3 * NSA_HEADS),
                              jnp.zeros((DM, GATE_ROWS - 3 * NSA_HEADS), w_in.dtype)], axis=1)
    wf = w_feat.T.astype(BF16)
    w_tok = jnp.concatenate([cols(nkv0 + 2 * kvw, kvw), cols(nkv0 + 4 * kvw, kvw),
                             cols(nkv0, kvw), cols(nkv0 + kvw, kvw),
                             cols(gq0, GLA_QK_W), cols(gk0, GLA_QK_W), cols(gv0, GLA_V_W),
                             cols(gout0, GLA_V_W), cols(glr0, GLA_GATE_RANK),
                             jnp.zeros((DM, LANES - GLA_GATE_RANK), w_in.dtype)], axis=1)
    wt = w_tok.astype(BF16)
    nf, nt = wf.shape[0], wt.shape[1]
    wg2 = jnp.concatenate([w_gate2, jnp.zeros((LANES - GLA_GATE_RANK, GLA_QK_W), w_gate2.dtype)],
                          axis=0).astype(BF16)
    cos, sin, rc, ra, rb = _rope_tables(np.arange(S))
    cosT, sinT = jnp.asarray(cos.T), jnp.asarray(sin.T)
    cmp_end = np.arange(nc) * CMP_STRIDE + CMP_BLOCK - 1
    _, _, cc, ca, cb = _rope_tables(cmp_end)
    bd = np.kron(np.eye(LANES // D, dtype=np.float32), np.full((D, D), 1.0 / D, np.float32))
    tile2 = lambda v: jnp.tile(v.reshape(1, D), (1, LANES // D))

    x2 = x.reshape(B * S, DM)
    tok = lambda w: pl.BlockSpec((tm, w), lambda s, b: (b * n_s + s, 0))
    feat = lambda r: pl.BlockSpec((1, r, tm), lambda s, b: (b, 0, s))
    stab = lambda r, w: pl.BlockSpec((r, w), lambda s, b: (0, s)) if r == ROT_HALF else \
        pl.BlockSpec((r, w), lambda s, b: (s, 0))
    outs = pl.pallas_call(
        _in_proj_kernel,
        grid=(n_s, B),
        in_specs=[tok(DM), _const_spec((1, DM)), _const_spec((nf, DM)), _const_spec((DM, nt)),
                  _const_spec((D, 1)), _const_spec((1, LANES)), _const_spec((1, LANES)),
                  _const_spec((LANES, LANES)),
                  stab(ROT_HALF, tm), stab(ROT_HALF, tm), stab(tm, LANES), stab(tm, LANES), stab(tm, LANES),
                  _const_spec((LANES, GLA_QK_W)), _const_spec((1, GLA_QK_W))],
        out_specs=[feat(NSA_Q_W), pl.BlockSpec((1, 2, kvw, tm), lambda s, b: (b, 0, 0, s)), feat(GATE_ROWS),
                   pl.BlockSpec((1, 2, tm, 2 * LANES), lambda s, b: (b, 0, s, 0)),
                   tok(kvw), tok(kvw), tok(GLA_QK_W), tok(GLA_QK_W), tok(GLA_QK_W),
                   tok(GLA_V_W), tok(GLA_V_W)],
        out_shape=[jax.ShapeDtypeStruct((B, NSA_Q_W, S), BF16), jax.ShapeDtypeStruct((B, 2, kvw, S), BF16),
                   jax.ShapeDtypeStruct((B, GATE_ROWS, S), F32),
                   jax.ShapeDtypeStruct((B, 2, S, 2 * LANES), BF16),
                   jax.ShapeDtypeStruct((B * S, kvw), F32), jax.ShapeDtypeStruct((B * S, kvw), F32),
                   jax.ShapeDtypeStruct((B * S, GLA_QK_W), F32), jax.ShapeDtypeStruct((B * S, GLA_QK_W), F32),
                   jax.ShapeDtypeStruct((B * S, GLA_QK_W), F32),
                   jax.ShapeDtypeStruct((B * S, GLA_V_W), BF16), jax.ShapeDtypeStruct((B * S, GLA_V_W), BF16)],
        compiler_params=pltpu.CompilerParams(dimension_semantics=("arbitrary", "arbitrary"),
                                             vmem_limit_bytes=VMEM_LIMIT),
        name="in_proj",
    )(x2, attn_norm.reshape(1, DM), wf, wt, q_gain.reshape(D, 1), tile2(k_gains[1]), tile2(k_gains[2]),
      jnp.asarray(bd, BF16), cosT, sinT, jnp.asarray(rc), jnp.asarray(ra), jnp.asarray(rb),
      wg2, b_gate.reshape(1, GLA_QK_W))
    qT, vcat, gateT, kcat, kc, vc, gq, gk, la, gv, gout = outs

    seq = lambda a: a.reshape(B, S, kvw)
    pad_w2 = lambda w: jnp.concatenate([w, jnp.zeros((CMP_HIDDEN, LANES - D), w.dtype)], axis=1).astype(BF16)
    pe_rows = lambda p: jnp.tile(p.reshape(1, CMP_BLOCK * D), (SUBLANES, 1)).astype(BF16)
    cin = CMP_BLOCK * D
    kcmp, vcT = pl.pallas_call(
        functools.partial(_compress_kernel, nc=nc),
        grid=(B,),
        in_specs=[pl.BlockSpec((1, S, kvw), lambda b: (b, 0, 0)), pl.BlockSpec((1, S, kvw), lambda b: (b, 0, 0)),
                  _const_spec((cin, CMP_HIDDEN)), _const_spec((CMP_HIDDEN, LANES)),
                  _const_spec((cin, CMP_HIDDEN)), _const_spec((CMP_HIDDEN, LANES)),
                  _const_spec((SUBLANES, cin)), _const_spec((SUBLANES, cin)), _const_spec((1, LANES)),
                  _const_spec((nc, LANES)), _const_spec((nc, LANES)), _const_spec((nc, LANES))],
        out_specs=[pl.BlockSpec((1, G, nc, LANES), lambda b: (b, 0, 0, 0)),
                   pl.BlockSpec((1, G, D, nc), lambda b: (b, 0, 0, 0))],
        out_shape=[jax.ShapeDtypeStruct((B, G, nc, LANES), BF16), jax.ShapeDtypeStruct((B, G, D, nc), BF16)],
        compiler_params=pltpu.CompilerParams(dimension_semantics=("arbitrary",), vmem_limit_bytes=VMEM_LIMIT),
        name="compress",
    )(seq(kc), seq(vc), w1_k.astype(BF16), pad_w2(w2_k), w1_v.astype(BF16), pad_w2(w2_v), pe_rows(pe_k), pe_rows(pe_v),
      tile2(k_gains[0]), jnp.asarray(cc), jnp.asarray(ca), jnp.asarray(cb))

    ci = np.arange(nc)[None, :]
    sj = np.arange(ns)[:, None]
    ovl = ((ci * CMP_STRIDE < (sj + 1) * SLC_BLOCK) & (ci * CMP_STRIDE + CMP_BLOCK > sj * SLC_BLOCK)
           & (ci < n_cmp)).astype(np.float32)
    n_q = S // tq
    wq = NSA_HEADS * tq
    o_nsa = pl.pallas_call(
        functools.partial(_nsa_kernel, tq_size=tq, nc=nc, n_cmp=n_cmp, ns=ns, top_k=top_k),
        grid=(B, n_q),
        in_specs=[pl.BlockSpec((1, NSA_Q_W, tq), lambda b, i: (b, 0, i)),
                  pl.BlockSpec((1, G, nc, LANES), lambda b, i: (b, 0, 0, 0)),
                  pl.BlockSpec((1, G, D, nc), lambda b, i: (b, 0, 0, 0)),
                  _const_spec((ns, nc)),
                  pl.BlockSpec((1, 2, S, 2 * LANES), lambda b, i: (b, 0, 0, 0)),
                  pl.BlockSpec((1, 2, kvw, S), lambda b, i: (b, 0, 0, 0)),
                  pl.BlockSpec((1, GATE_ROWS, tq), lambda b, i: (b, 0, i))],
        out_specs=pl.BlockSpec((1, tq, NSA_Q_W), lambda b, i: (b, i, 0)),
        out_shape=jax.ShapeDtypeStruct((B, S, NSA_Q_W), BF16),
        scratch_shapes=[pltpu.VMEM((NSA_HEADS, 2 * D + LANES, tq), BF16),
                        pltpu.VMEM((NSA_HEADS, tq, tq), BF16),
                        pltpu.VMEM((NSA_HEADS, 1, tq), F32),
                        pltpu.VMEM((2, NSA_HEADS, D + 2 * SUBLANES, tq), F32),
                        pltpu.VMEM((2, NSA_HEADS, 1, tq), F32),
                        pltpu.VMEM((NSA_HEADS, D, tq), F32)],
        compiler_params=pltpu.CompilerParams(dimension_semantics=("arbitrary", "arbitrary"),
                                             vmem_limit_bytes=VMEM_LIMIT),
        name="nsa_attn",
    )(qT, kcmp, vcT, jnp.asarray(ovl, BF16), kcat, vcat, gateT)

    g3 = lambda a: a.reshape(B, S, a.shape[-1])
    gspec = lambda w: pl.BlockSpec((1, tg, w), lambda b, t: (b, t, 0))
    hmid = pl.pallas_call(
        functools.partial(_gla_out_kernel, tg=tg),
        grid=(B, S // tg),
        in_specs=[gspec(GLA_QK_W), gspec(GLA_QK_W), gspec(GLA_QK_W), gspec(GLA_V_W), gspec(GLA_V_W),
                  _const_spec((1, GLA_DV)), gspec(DM), gspec(NSA_Q_W), _const_spec((NSA_Q_W + GLA_V_W, DM))],
        out_specs=gspec(DM),
        out_shape=jax.ShapeDtypeStruct((B, S, DM), F32),
        scratch_shapes=[pltpu.VMEM((GLA_HEADS, GLA_DV, GLA_QK_W), F32)],
        compiler_params=pltpu.CompilerParams(dimension_semantics=("arbitrary", "arbitrary"),
                                             vmem_limit_bytes=VMEM_LIMIT),
        name="gla_out",
    )(g3(gq), g3(gk), g3(la), g3(gv), g3(gout), out_gain.reshape(1, GLA_DV), x, o_nsa, w_out.astype(BF16))
    hmid = hmid.reshape(B * S, DM)
    tmf = min(1024, S)
    n_t = (B * S) // tmf
    row = lambda w: pl.BlockSpec((tmf, w), lambda t: (t, 0))

    tf = 256
    assert D_FF % tf == 0
    halo_blocks = tmf // SUBLANES
    out = pl.pallas_call(
        functools.partial(_ffn_kernel, tm=tmf, tf=tf, tiles_per_seq=S // tmf),
        grid=(n_t,),
        in_specs=[row(DM),
                  pl.BlockSpec((SUBLANES, DM), lambda t: (jnp.maximum(t * halo_blocks - 1, 0), 0)),
                  _const_spec((1, DM)),
                  pl.BlockSpec((DM, 2 * D_FF), lambda t: (0, 0), pipeline_mode=pl.Buffered(1)),
                  _const_spec((CONV_W, 2 * D_FF)), _const_spec((1, 2 * D_FF)),
                  pl.BlockSpec((D_FF, DM), lambda t: (0, 0), pipeline_mode=pl.Buffered(1))],
        out_specs=row(DM),
        out_shape=jax.ShapeDtypeStruct((B * S, DM), F32),
        scratch_shapes=[pltpu.VMEM((4, tmf + SUBLANES, tf), F32), pltpu.VMEM((tmf, D_FF), BF16)],
        compiler_params=pltpu.CompilerParams(dimension_semantics=("arbitrary",), vmem_limit_bytes=VMEM_LIMIT),
        name="ffn",
    )(hmid, hmid, ffn_norm.reshape(1, DM), w_up.astype(BF16), conv_w, conv_b.reshape(1, 2 * D_FF),
      w_down.astype(BF16))
    return out.reshape(B, S, DM)
```
